```python
import math, functools
import jax, jax.numpy as jnp
from jax import lax
import numpy as np

D_MODEL = 4096
BATCH = 4
SEQ = 2048
DEPTH = 4
DEC_BATCH = 128
DEC_SEQ = 8
PAST_LEN = 8192
PAGE_SIZE = 128

HEAD_DIM = 128
H_A = D_MODEL // (2 * HEAD_DIM)
H_IDX = 32
IDX_DIM = 64
DSA_TOPK = 256
H_B = D_MODEL // (2 * HEAD_DIM)
Q_LORA = 768
KV_LORA = 512
NOPE_DIM = 128
ROPE_DIM = 64
V_DIM = 128
ROPE_THETA = 10000.0
H_C = D_MODEL // HEAD_DIM
FORGET_BIAS = 3.0
N_BUCKETS = 32
MAX_DISTANCE = 128
D_FF = 6144
N_EXPERTS = 8
TOP_K = 2
D_FF_EXPERT = 2048
PLE_DIM = 256
Q_BLOCK = 128
LN_EPS = 1e-5
RMS_EPS = 1e-6
N_EVEN = (DEPTH + 1) // 2
N_ODD = DEPTH // 2
DEEPNORM_ALPHA = (2 * DEPTH) ** 0.25
DEEPNORM_BETA = (8 * DEPTH) ** -0.25
DSA_ROW = 2 * HEAD_DIM + IDX_DIM
MLA_ROW = KV_LORA + ROPE_DIM
AB_SIZES = (H_A * HEAD_DIM, HEAD_DIM, HEAD_DIM, H_IDX * IDX_DIM, IDX_DIM, H_IDX, Q_LORA, KV_LORA, ROPE_DIM)
C_SIZES = (H_C * HEAD_DIM, HEAD_DIM, HEAD_DIM, H_C)
AB_COLS = sum(AB_SIZES)
C_COLS = sum(C_SIZES)
ATTN_SCALE = HEAD_DIM ** -0.5
MLA_SCALE = (NOPE_DIM + ROPE_DIM) ** -0.5
IDX_W_SCALE = (H_IDX * IDX_DIM) ** -0.5

kernel_name = 'hybrid_dsa_mla_fox_decoder_step'


def layer_norm(x, g, b):
    xf = x.astype(jnp.float32)
    mu = jnp.mean(xf, -1, keepdims=True)
    xc = xf - mu
    var = jnp.mean(xc * xc, -1, keepdims=True)
    return (xc * lax.rsqrt(var + LN_EPS) * g + b).astype(x.dtype)


def rms_norm(x, g):
    xf = x.astype(jnp.float32)
    return (xf * lax.rsqrt(jnp.mean(xf * xf, -1, keepdims=True) + RMS_EPS) * g).astype(x.dtype)


def split_cols(z, sizes):
    parts, off = [], 0
    for s in sizes:
        parts.append(z[..., off:off + s])
        off += s
    return parts


def rope_tables(pos):
    inv = ROPE_THETA ** (-jnp.arange(0, ROPE_DIM, 2, dtype=jnp.float32) / ROPE_DIM)
    ang = pos.astype(jnp.float32)[:, None] * inv[None, :]
    return jnp.cos(ang), jnp.sin(ang)


def apply_rope(x, cos, sin):
    xf = x.astype(jnp.float32)
    x1, x2 = xf[..., :ROPE_DIM // 2], xf[..., ROPE_DIM // 2:]
    return jnp.concatenate([x1 * cos - x2 * sin, x1 * sin + x2 * cos], -1).astype(x.dtype)


def rel_bucket(dist):
    max_exact = N_BUCKETS // 2
    d = jnp.maximum(dist, 0)
    ratio = jnp.log(jnp.maximum(d, 1).astype(jnp.float32) / max_exact) / math.log(MAX_DISTANCE / max_exact)
    large = jnp.minimum(max_exact + (ratio * (N_BUCKETS - max_exact)).astype(jnp.int32), N_BUCKETS - 1)
    return jnp.where(d < max_exact, d, large)


def map_query_blocks(block_fn, q_args, q_pos):
    tq = q_pos.shape[0]
    qb = Q_BLOCK if tq % Q_BLOCK == 0 else tq
    nb = tq // qb
    xs = tuple(a.reshape((nb, qb) + a.shape[1:]) for a in q_args) + (q_pos.reshape(nb, qb),)
    out = lax.map(lambda a: block_fn(*a), xs)
    return jax.tree_util.tree_map(lambda o: o.reshape((tq,) + o.shape[2:]), out)


def ab_sequence(q_a, q_i, w_i, q_lat, q_rope, dsa_rows, mla_rows, q_pos, rel_bias):
    n_keys = dsa_rows.shape[0]
    topk = min(DSA_TOPK, n_keys // 4)
    k_pos = jnp.arange(n_keys, dtype=jnp.int32)
    k_a, v_a, k_i = split_cols(dsa_rows, (HEAD_DIM, HEAD_DIM, IDX_DIM))
    c_kv, k_r = split_cols(mla_rows, (KV_LORA, ROPE_DIM))

    def block(qa, qi, wi, ql, qr, pos):
        causal = k_pos[None, :] <= pos[:, None]
        act = jax.nn.relu(jnp.einsum('thd,sd->ths', qi, k_i).astype(jnp.float32))
        score = jnp.einsum('th,ths->ts', wi.astype(jnp.float32) * IDX_W_SCALE, act)
        score = jnp.where(causal, score, -jnp.inf)
        _, sel = lax.top_k(score, topk)
        valid = sel <= pos[:, None]
        k_sel, v_sel = k_a[sel], v_a[sel]
        bias = rel_bias[rel_bucket(pos[:, None] - sel)].astype(jnp.float32)
        logit = jnp.einsum('thd,tkd->thk', qa, k_sel).astype(jnp.float32) * ATTN_SCALE + jnp.swapaxes(bias, 1, 2)
        logit = jnp.where(valid[:, None, :], logit, -jnp.inf)
        o_a = jnp.einsum('thk,tkd->thd', jax.nn.softmax(logit, -1).astype(v_sel.dtype), v_sel)
        s = (jnp.einsum('thc,sc->ths', ql, c_kv) + jnp.einsum('thr,sr->ths', qr, k_r)).astype(jnp.float32) * MLA_SCALE
        s = jnp.where(causal[:, None, :], s, -jnp.inf)
        o_lat = jnp.einsum('ths,sc->thc', jax.nn.softmax(s, -1).astype(c_kv.dtype), c_kv)
        return o_a, o_lat

    return map_query_blocks(block, (q_a, q_i, w_i, q_lat, q_rope), q_pos)


def fox_sequence(q, kv_rows, logf_rows, q_pos):
    n_keys = kv_rows.shape[0]
    k_pos = jnp.arange(n_keys, dtype=jnp.int32)
    k, v = split_cols(kv_rows, (HEAD_DIM, HEAD_DIM))
    cum = jnp.cumsum(logf_rows.astype(jnp.float32), axis=0)
    cum_q = cum[n_keys - q.shape[0]:]
    cum_k = cum.T

    def block(qb, cq, pos):
        s = jnp.einsum('thd,sd->hts', qb, k).astype(jnp.float32) * ATTN_SCALE + cq.T[:, :, None] - cum_k[:, None, :]
        s = jnp.where((k_pos[None, :] <= pos[:, None])[None], s, -jnp.inf)
        return jnp.einsum('hts,sd->thd', jax.nn.softmax(s, -1).astype(v.dtype), v)

    return map_query_blocks(block, (q, cum_q), q_pos)


def map_sequences(seq_fn, queries, rows, pools, layer, page_table):
    n_q = len(queries)
    if pools is None:
        return lax.map(lambda a: seq_fn(*a), queries + rows)

    def body(a):
        pages = a[-1]
        full = tuple(jnp.concatenate([pool[layer, pages].reshape(-1, pool.shape[-1]), new], axis=0)
                     for pool, new in zip(pools, a[n_q:-1]))
        return seq_fn(*a[:n_q], *full)

    return lax.map(body, queries + rows + (page_table,))


def ab_project(x, cos, sin, w_in, q_norm, w_uq, kv_norm, w_uk):
    n, t, _ = x.shape
    q_a, k_a, v_a, q_i, k_i, w_i, c_q, c_kv, k_r = split_cols(x @ w_in, AB_SIZES)
    q_a = q_a.reshape(n, t, H_A, HEAD_DIM)
    q_i = q_i.reshape(n, t, H_IDX, IDX_DIM)
    q_b = jnp.einsum('ntc,chd->nthd', rms_norm(c_q, q_norm), w_uq)
    q_nope, q_rope = q_b[..., :NOPE_DIM], q_b[..., NOPE_DIM:]
    q_rope = apply_rope(q_rope, cos[:, None, :], sin[:, None, :])
    q_lat = jnp.einsum('nthd,chd->nthc', q_nope, w_uk)
    c_kv = rms_norm(c_kv, kv_norm)
    k_r = apply_rope(k_r, cos, sin)
    dsa_rows = jnp.concatenate([k_a, v_a, k_i], -1)
    mla_rows = jnp.concatenate([c_kv, k_r], -1)
    return (q_a, q_i, w_i, q_lat, q_rope), (dsa_rows, mla_rows)


def fox_project(x, w_in, b_f):
    n, t, _ = x.shape
    q, k, v, f = split_cols(x @ w_in, C_SIZES)
    logf = jax.nn.log_sigmoid((f + b_f).astype(jnp.float32)).astype(x.dtype)
    return (q.reshape(n, t, H_C, HEAD_DIM),), (jnp.concatenate([k, v], -1), logf)


def swiglu(h, w_up, w_down):
    gate, up = jnp.split(h @ w_up, 2, axis=-1)
    return (jax.nn.silu(gate) * up) @ w_down


def moe_swiglu(h, w_router, w_up, w_down):
    logits = (h @ w_router).astype(jnp.float32)
    top_v, top_i = lax.top_k(logits, TOP_K)
    gates = jax.nn.softmax(top_v, axis=-1)
    combine = jnp.einsum('ntk,ntke->nte', gates, jax.nn.one_hot(top_i, N_EXPERTS, dtype=jnp.float32)).astype(h.dtype)
    out = jnp.zeros_like(h)
    for e in range(N_EXPERTS):
        out = out + combine[..., e:e + 1] * swiglu(h, w_up[e], w_down[e])
    return out


def trunk(x, ple, q_pos, pools_ab, pools_c, page_table, w):
    n, t, _ = x.shape
    cos, sin = rope_tables(q_pos)
    ab_fn = functools.partial(ab_sequence, q_pos=q_pos, rel_bias=w['rel_bias'])
    c_fn = functools.partial(fox_sequence, q_pos=q_pos)
    new_dsa, new_mla, new_fkv, new_flf = [], [], [], []
    for i in range(DEPTH):
        j = i // 2
        if i % 2 == 0:
            queries, rows = ab_project(x, cos, sin, w['w_in_ab'][j], w['mla_q_norm'][j], w['w_uq'][j],
                                       w['mla_kv_norm'][j], w['w_uk'][j])
            o_a, o_lat = map_sequences(ab_fn, queries, rows, pools_ab, j, page_table)
            o_b = jnp.einsum('nthc,chd->nthd', o_lat, w['w_uv'][j])
            mix = jnp.concatenate([o_a.reshape(n, t, -1), o_b.reshape(n, t, -1)], -1) @ w['w_o_ab'][j]
            new_dsa.append(rows[0])
            new_mla.append(rows[1])
        else:
            queries, rows = fox_project(x, w['w_in_c'][j], w['b_forget'][j])
            o_c = map_sequences(c_fn, queries, rows, pools_c, j, page_table)
            mix = o_c.reshape(n, t, -1) @ w['w_o_c'][j]
            new_fkv.append(rows[0])
            new_flf.append(rows[1])
        h = layer_norm(DEEPNORM_ALPHA * x + mix, w['ln1_g'][i], w['ln1_b'][i])
        if i % 2 == 0:
            ff = swiglu(h, w['w_ffn_up'][j], w['w_ffn_down'][j])
        else:
            ff = moe_swiglu(h, w['w_router'][j], w['w_moe_up'][j], w['w_moe_down'][j])
        h = layer_norm(DEEPNORM_ALPHA * h + ff, w['ln2_g'][i], w['ln2_b'][i])
        gate = jax.nn.sigmoid(h @ w['w_ple_gate'][i] + w['b_ple_gate'][i])
        x = h + gate * (ple[i] @ w['w_ple_proj'][i])
    return x, jnp.stack(new_dsa), jnp.stack(new_mla), jnp.stack(new_fkv), jnp.stack(new_flf)


def setup_inputs(seed: int = 0) -> dict:
    key = jax.random.key(seed)
    ks = iter(jax.random.split(key, 48))

    def nrm(shape, scale=1.0):
        a = jax.random.normal(next(ks), shape, jnp.float32)
        return a if scale == 1.0 else a * scale

    n_pages = PAST_LEN // PAGE_SIZE
    n_used = DEC_BATCH * n_pages
    n_pool = n_used + n_used // 4
    d_in = D_MODEL ** -0.5
    return {
        'x_prompt': nrm((BATCH, SEQ, D_MODEL)),
        'x_sample': nrm((DEC_BATCH, DEC_SEQ, D_MODEL)),
        'cache_dsa': nrm((N_EVEN, n_pool, PAGE_SIZE, DSA_ROW)),
        'cache_mla': nrm((N_EVEN, n_pool, PAGE_SIZE, MLA_ROW)),
        'cache_fox_kv': nrm((N_ODD, n_pool, PAGE_SIZE, 2 * HEAD_DIM)),
        'cache_fox_logf': jax.nn.log_sigmoid(FORGET_BIAS + nrm((N_ODD, n_pool, PAGE_SIZE, H_C), 0.1)),
        'page_table': jax.random.permutation(next(ks), n_pool)[:n_used].reshape(DEC_BATCH, n_pages).astype(jnp.int32),
        'p_prompt': nrm((DEPTH, BATCH, SEQ, PLE_DIM)),
        'p_sample': nrm((DEPTH, DEC_BATCH, DEC_SEQ, PLE_DIM)),
        'rel_bias': nrm((N_BUCKETS, H_A), 0.5),
        'w_in_ab': nrm((N_EVEN, D_MODEL, AB_COLS), d_in),
        'mla_q_norm': 1.0 + nrm((N_EVEN, Q_LORA), 0.01),
        'w_uq': nrm((N_EVEN, Q_LORA, H_B, NOPE_DIM + ROPE_DIM), Q_LORA ** -0.5),
        'mla_kv_norm': 1.0 + nrm((N_EVEN, KV_LORA), 0.01),
        'w_uk': nrm((N_EVEN, KV_LORA, H_B, NOPE_DIM), KV_LORA ** -0.5),
        'w_uv': nrm((N_EVEN, KV_LORA, H_B, V_DIM), KV_LORA ** -0.5),
        'w_o_ab': nrm((N_EVEN, H_A * HEAD_DIM + H_B * V_DIM, D_MODEL), DEEPNORM_BETA * (H_A * HEAD_DIM + H_B * V_DIM) ** -0.5),
        'w_in_c': jnp.concatenate([nrm((N_ODD, D_MODEL, C_COLS - H_C), d_in), nrm((N_ODD, D_MODEL, H_C), 0.1 * d_in)], -1),
        'b_forget': FORGET_BIAS + nrm((N_ODD, H_C), 0.1),
        'w_o_c': nrm((N_ODD, H_C * HEAD_DIM, D_MODEL), DEEPNORM_BETA * (H_C * HEAD_DIM) ** -0.5),
        'ln1_g': 1.0 + nrm((DEPTH, D_MODEL), 0.01),
        'ln1_b': nrm((DEPTH, D_MODEL), 0.01),
        'ln2_g': 1.0 + nrm((DEPTH, D_MODEL), 0.01),
        'ln2_b': nrm((DEPTH, D_MODEL), 0.01),
        'w_ffn_up': nrm((N_EVEN, D_MODEL, 2 * D_FF), d_in),
        'w_ffn_down': nrm((N_EVEN, D_FF, D_MODEL), DEEPNORM_BETA * D_FF ** -0.5),
        'w_router': nrm((N_ODD, D_MODEL, N_EXPERTS), d_in),
        'w_moe_up': nrm((N_ODD, N_EXPERTS, D_MODEL, 2 * D_FF_EXPERT), d_in),
        'w_moe_down': nrm((N_ODD, N_EXPERTS, D_FF_EXPERT, D_MODEL), DEEPNORM_BETA * D_FF_EXPERT ** -0.5),
        'w_ple_gate': nrm((DEPTH, D_MODEL, D_MODEL), d_in),
        'b_ple_gate': nrm((DEPTH, D_MODEL), 0.01),
        'w_ple_proj': nrm((DEPTH, PLE_DIM, D_MODEL), PLE_DIM ** -0.5),
    }


def reference(x_prompt, x_sample, cache_dsa, cache_mla, cache_fox_kv, cache_fox_logf, page_table,
              p_prompt, p_sample, rel_bias, w_in_ab, mla_q_norm, w_uq, mla_kv_norm, w_uk, w_uv, w_o_ab,
              w_in_c, b_forget, w_o_c, ln1_g, ln1_b, ln2_g, ln2_b, w_ffn_up, w_ffn_down,
              w_router, w_moe_up, w_moe_down, w_ple_gate, b_ple_gate, w_ple_proj):
    w = dict(rel_bias=rel_bias, w_in_ab=w_in_ab, mla_q_norm=mla_q_norm, w_uq=w_uq, mla_kv_norm=mla_kv_norm,
             w_uk=w_uk, w_uv=w_uv, w_o_ab=w_o_ab, w_in_c=w_in_c, b_forget=b_forget, w_o_c=w_o_c,
             ln1_g=ln1_g, ln1_b=ln1_b, ln2_g=ln2_g, ln2_b=ln2_b, w_ffn_up=w_ffn_up, w_ffn_down=w_ffn_down,
             w_router=w_router, w_moe_up=w_moe_up, w_moe_down=w_moe_down, w_ple_gate=w_ple_gate,
             b_ple_gate=b_ple_gate, w_ple_proj=w_ple_proj)
    past_len = page_table.shape[1] * cache_dsa.shape[2]
    pos_prompt = jnp.arange(x_prompt.shape[1], dtype=jnp.int32)
    pos_sample = past_len + jnp.arange(x_sample.shape[1], dtype=jnp.int32)
    y_prompt, new_dsa_prompt, new_mla_prompt, new_fox_kv_prompt, new_fox_logf_prompt = trunk(
        x_prompt, p_prompt, pos_prompt, None, None, None, w)
    y_sample, new_dsa_sample, new_mla_sample, new_fox_kv_sample, new_fox_logf_sample = trunk(
        x_sample, p_sample, pos_sample, (cache_dsa, cache_mla), (cache_fox_kv, cache_fox_logf), page_table, w)
    return (y_prompt, y_sample, new_dsa_prompt, new_mla_prompt, new_fox_kv_prompt, new_fox_logf_prompt,
            new_dsa_sample, new_mla_sample, new_fox_kv_sample, new_fox_logf_sample)
```

```python
import functools
import math

import numpy as np
import jax
import jax.numpy as jnp
from jax import lax
from jax.experimental import pallas as pl
from jax.experimental.pallas import tpu as pltpu

DSA_TOPK = 256
N_BUCKETS = 32
MAX_DISTANCE = 128
ROPE_THETA = 10000.0
LN_EPS = 1e-5
RMS_EPS = 1e-6
MOE_TOP_K = 2

MXU_DTYPE = jnp.bfloat16
MASK_VALUE = -1e30
V7X_VMEM_BYTES = 64 << 20
VMEM_CAP_BYTES = V7X_VMEM_BYTES - (6 << 20)
LANES = 128
F32 = jnp.float32


def _tile(n, target, align=8):
    if n <= target:
        return n
    for d in range(target, 0, -1):
        if n % d == 0 and d % align == 0:
            return d
    return n


def _params(semantics, vmem_bytes):
    limit = int(min(max(vmem_bytes, 16 << 20), VMEM_CAP_BYTES))
    return pltpu.CompilerParams(dimension_semantics=semantics, vmem_limit_bytes=limit)


def _nt_dot(a, b):
    return lax.dot_general(a, b, (((1,), (1,)), ((), ())), preferred_element_type=F32)


def _dot(a, b):
    return jnp.dot(a, b, preferred_element_type=F32)


def _mm_kernel(x_ref, w_ref, o_ref, *scratch, nk):
    part = _dot(x_ref[...].astype(MXU_DTYPE), w_ref[...].astype(MXU_DTYPE))
    if nk == 1:
        o_ref[...] = part.astype(o_ref.dtype)
        return
    acc_ref, = scratch
    k = pl.program_id(2)

    @pl.when(k == 0)
    def _():
        acc_ref[...] = part

    @pl.when(k > 0)
    def _():
        acc_ref[...] += part

    @pl.when(k == nk - 1)
    def _():
        o_ref[...] = acc_ref[...].astype(o_ref.dtype)


def matmul(x, w, *, out_dtype, col_start=0, n_cols=None, tm=1024, tn=256, tk=4096):
    m, kdim = x.shape
    n_cols = w.shape[1] - col_start if n_cols is None else n_cols
    tm = _tile(m, tm)
    tk = _tile(kdim, tk, LANES)
    if n_cols % LANES == 0 and col_start % LANES == 0:
        tn = _tile(n_cols, tn, LANES)
        while col_start % tn:
            tn -= LANES
    else:
        assert col_start == 0 and n_cols == w.shape[1]
        tn = n_cols
    off = col_start // tn
    xb, ob = x.dtype.itemsize, jnp.dtype(out_dtype).itemsize

    def estimate(tk):
        return 2 * (tm * tk * xb + tk * tn * 4 + tm * tn * ob) + tm * tn * 4 + tk * tn * 2 + tm * tk * 2 + (4 << 20)

    while estimate(tk) > VMEM_CAP_BYTES * 3 // 4 and tk % (2 * LANES) == 0:
        tk //= 2
    nk = kdim // tk
    vmem = estimate(tk)
    return pl.pallas_call(
        functools.partial(_mm_kernel, nk=nk),
        grid=(m // tm, n_cols // tn, nk),
        in_specs=[pl.BlockSpec((tm, tk), lambda i, j, k: (i, k)),
                  pl.BlockSpec((tk, tn), lambda i, j, k: (k, j + off))],
        out_specs=pl.BlockSpec((tm, tn), lambda i, j, k: (i, j)),
        out_shape=jax.ShapeDtypeStruct((m, n_cols), out_dtype),
        scratch_shapes=[pltpu.VMEM((tm, tn), F32)] if nk > 1 else [],
        compiler_params=_params(("parallel", "parallel", "arbitrary"), vmem),
    )(x, w)


def _swiglu_up_kernel(x_ref, wg_ref, wu_ref, o_ref):
    x = x_ref[...]
    gate = _dot(x, wg_ref[...].astype(MXU_DTYPE))
    up = _dot(x, wu_ref[...].astype(MXU_DTYPE))
    o_ref[...] = (gate * jax.nn.sigmoid(gate) * up).astype(o_ref.dtype)


def swiglu_up(xb, w_up, *, tm=1024, tn=256):
    m, d = xb.shape
    f = w_up.shape[1] // 2
    tm, tn = _tile(m, tm), _tile(f, tn, LANES)
    nf = f // tn
    vmem = 2 * (tm * d * 2 + 2 * d * tn * 4 + tm * tn * 2) + 2 * d * tn * 2 + 3 * tm * tn * 4 + (4 << 20)
    return pl.pallas_call(
        _swiglu_up_kernel,
        grid=(m // tm, nf),
        in_specs=[pl.BlockSpec((tm, d), lambda i, j: (i, 0)),
                  pl.BlockSpec((d, tn), lambda i, j: (0, j)),
                  pl.BlockSpec((d, tn), lambda i, j: (0, j + nf))],
        out_specs=pl.BlockSpec((tm, tn), lambda i, j: (i, j)),
        out_shape=jax.ShapeDtypeStruct((m, f), MXU_DTYPE),
        compiler_params=_params(("parallel", "parallel"), vmem),
    )(xb, w_up, w_up)


def moe_up(xb, w_up, *, tm=1024, tn=256):
    m, d = xb.shape
    n_exp, _, f2 = w_up.shape
    f = f2 // 2
    tm, tn = _tile(m, tm), _tile(f, tn, LANES)
    nf = f // tn
    vmem = 2 * (tm * d * 2 + 2 * d * tn * 4 + tm * tn * 2) + 2 * d * tn * 2 + 3 * tm * tn * 4 + (4 << 20)
    return pl.pallas_call(
        _swiglu_up_kernel,
        grid=(m // tm, n_exp, nf),
        in_specs=[pl.BlockSpec((tm, d), lambda i, e, j: (i, 0)),
                  pl.BlockSpec((None, d, tn), lambda i, e, j: (e, 0, j)),
                  pl.BlockSpec((None, d, tn), lambda i, e, j: (e, 0, j + nf))],
        out_specs=pl.BlockSpec((tm, tn), lambda i, e, j: (i, e * nf + j)),
        out_shape=jax.ShapeDtypeStruct((m, n_exp * f), MXU_DTYPE),
        compiler_params=_params(("parallel", "parallel", "parallel"), vmem),
    )(xb, w_up, w_up)


def _moe_down_kernel(a_ref, w_ref, c_ref, o_ref, acc_ref, *, n_exp):
    e = pl.program_id(2)
    part = c_ref[...] * _dot(a_ref[...], w_ref[...].astype(MXU_DTYPE))

    @pl.when(e == 0)
    def _():
        acc_ref[...] = part

    @pl.when(e > 0)
    def _():
        acc_ref[...] += part

    @pl.when(e == n_exp - 1)
    def _():
        o_ref[...] = acc_ref[...]


def moe_down(act, w_down, combine_t, *, tm=1024, tn=512):
    m = act.shape[0]
    n_exp, f, d = w_down.shape
    tm, tn = _tile(m, tm), _tile(d, tn, LANES)
    vmem = 2 * (tm * f * 2 + f * tn * 4 + tm * LANES * 4 + tm * tn * 4) + 3 * tm * tn * 4 + f * tn * 2 + (4 << 20)
    return pl.pallas_call(
        functools.partial(_moe_down_kernel, n_exp=n_exp),
        grid=(m // tm, d // tn, n_exp),
        in_specs=[pl.BlockSpec((tm, f), lambda i, j, e: (i, e)),
                  pl.BlockSpec((None, f, tn), lambda i, j, e: (e, 0, j)),
                  pl.BlockSpec((None, tm, 1), lambda i, j, e: (e, i, 0))],
        out_specs=pl.BlockSpec((tm, tn), lambda i, j, e: (i, j)),
        out_shape=jax.ShapeDtypeStruct((m, d), F32),
        scratch_shapes=[pltpu.VMEM((tm, tn), F32)],
        compiler_params=_params(("parallel", "parallel", "arbitrary"), vmem),
    )(act, w_down, combine_t)


def _router_kernel(x_ref, w_ref, o_ref):
    logits = _dot(x_ref[...], w_ref[...].astype(MXU_DTYPE))
    n_exp = logits.shape[1]
    lane = lax.broadcasted_iota(jnp.int32, logits.shape, 1)
    m1 = jnp.max(logits, axis=1, keepdims=True)
    i1 = jnp.min(jnp.where(logits == m1, lane, n_exp), axis=1, keepdims=True)
    rest = jnp.where(lane == i1, -jnp.inf, logits)
    m2 = jnp.max(rest, axis=1, keepdims=True)
    i2 = jnp.min(jnp.where(rest == m2, lane, n_exp), axis=1, keepdims=True)
    e2 = jnp.exp(m2 - m1)
    denom = 1.0 + e2
    o_ref[...] = jnp.where(lane == i1, 1.0 / denom, 0.0) + jnp.where(lane == i2, e2 / denom, 0.0)


def router_combine(xb, w_router, *, tm=1024):
    m, d = xb.shape
    n_exp = w_router.shape[1]
    tm = _tile(m, tm)
    vmem = 2 * (tm * d * 2 + d * LANES * 4 + tm * LANES * 4) + (8 << 20)
    return pl.pallas_call(
        _router_kernel,
        grid=(m // tm,),
        in_specs=[pl.BlockSpec((tm, d), lambda i: (i, 0)),
                  pl.BlockSpec((d, n_exp), lambda i: (0, 0))],
        out_specs=pl.BlockSpec((tm, n_exp), lambda i: (i, 0)),
        out_shape=jax.ShapeDtypeStruct((m, n_exp), F32),
        compiler_params=_params(("parallel",), vmem),
    )(xb, w_router)


def _deepnorm_kernel(x_ref, y_ref, g_ref, b_ref, o_ref, ob_ref, *, alpha):
    z = alpha * x_ref[...] + y_ref[...]
    mu = jnp.mean(z, axis=-1, keepdims=True)
    zc = z - mu
    var = jnp.mean(zc * zc, axis=-1, keepdims=True)
    out = zc * lax.rsqrt(var + LN_EPS) * g_ref[...] + b_ref[...]
    o_ref[...] = out
    ob_ref[...] = out.astype(ob_ref.dtype)


def deepnorm(x, y, g, b, alpha, *, tm=256):
    m, d = x.shape
    tm = _tile(m, tm)
    row = pl.BlockSpec((tm, d), lambda i: (i, 0))
    vec = pl.BlockSpec((1, d), lambda i: (0, 0))
    return pl.pallas_call(
        functools.partial(_deepnorm_kernel, alpha=alpha),
        grid=(m // tm,),
        in_specs=[row, row, vec, vec],
        out_specs=[row, row],
        out_shape=[jax.ShapeDtypeStruct((m, d), F32), jax.ShapeDtypeStruct((m, d), MXU_DTYPE)],
        compiler_params=_params(("parallel",), 12 * tm * d * 4 + (4 << 20)),
    )(x, y, g.reshape(1, d), b.reshape(1, d))


def _ple_kernel(hb_ref, wg_ref, bg_ref, p_ref, wp_ref, h_ref, o_ref, ob_ref):
    gate = jax.nn.sigmoid(_dot(hb_ref[...], wg_ref[...].astype(MXU_DTYPE)) + bg_ref[...])
    emb = _dot(p_ref[...].astype(MXU_DTYPE), wp_ref[...].astype(MXU_DTYPE))
    out = h_ref[...] + gate * emb
    o_ref[...] = out
    ob_ref[...] = out.astype(ob_ref.dtype)


def ple_mix(h, hb, w_gate, b_gate, p, w_proj, *, tm=1024, tn=256):
    m, d = h.shape
    pd = p.shape[1]
    tm, tn = _tile(m, tm), _tile(d, tn, LANES)
    vmem = 2 * (tm * d * 2 + d * tn * 4 + tm * pd * 4 + pd * tn * 4 + 3 * tm * tn * 4) + d * tn * 2 + 4 * tm * tn * 4 + (4 << 20)
    tile = pl.BlockSpec((tm, tn), lambda i, j: (i, j))
    return pl.pallas_call(
        _ple_kernel,
        grid=(m // tm, d // tn),
        in_specs=[pl.BlockSpec((tm, d), lambda i, j: (i, 0)),
                  pl.BlockSpec((d, tn), lambda i, j: (0, j)),
                  pl.BlockSpec((1, tn), lambda i, j: (0, j)),
                  pl.BlockSpec((tm, pd), lambda i, j: (i, 0)),
                  pl.BlockSpec((pd, tn), lambda i, j: (0, j)),
                  tile],
        out_specs=[tile, tile],
        out_shape=[jax.ShapeDtypeStruct((m, d), F32), jax.ShapeDtypeStruct((m, d), MXU_DTYPE)],
        compiler_params=_params(("parallel", "parallel"), vmem),
    )(hb, w_gate, b_gate.reshape(1, d), p, w_proj, h)


def _rms(x, g):
    return x * lax.rsqrt(jnp.mean(x * x, axis=-1, keepdims=True) + RMS_EPS) * g


def _ab_post_kernel(cq_ref, ckv_ref, k1_ref, k2_ref, cos_ref, sin_ref, gq_ref, gkv_ref,
                    cqn_ref, ckvn_ref, r1_ref, r2_ref):
    cqn_ref[...] = _rms(cq_ref[...], gq_ref[...]).astype(cqn_ref.dtype)
    ckvn_ref[...] = _rms(ckv_ref[...], gkv_ref[...])
    x1, x2, cos, sin = k1_ref[...], k2_ref[...], cos_ref[...], sin_ref[...]
    r1_ref[...] = x1 * cos - x2 * sin
    r2_ref[...] = x1 * sin + x2 * cos


def ab_post(c_q, c_kv, k_r, cos, sin, q_norm, kv_norm, *, tm=512):
    m, ql = c_q.shape
    kl, half = c_kv.shape[1], cos.shape[1]
    tm = _tile(m, tm)

    def row(n):
        return pl.BlockSpec((tm, n), lambda i: (i, 0))

    def vec(n):
        return pl.BlockSpec((1, n), lambda i: (0, 0))

    cqn, ckvn, r1, r2 = pl.pallas_call(
        _ab_post_kernel,
        grid=(m // tm,),
        in_specs=[row(ql), row(kl), row(half), row(half), row(half), row(half), vec(ql), vec(kl)],
        out_specs=[row(ql), row(kl), row(half), row(half)],
        out_shape=[jax.ShapeDtypeStruct((m, ql), MXU_DTYPE), jax.ShapeDtypeStruct((m, kl), F32),
                   jax.ShapeDtypeStruct((m, half), F32), jax.ShapeDtypeStruct((m, half), F32)],
        compiler_params=_params(("parallel",), 32 << 20),
    )(c_q, c_kv, k_r[:, :half], k_r[:, half:], cos, sin, q_norm.reshape(1, ql), kv_norm.reshape(1, kl))
    return cqn, jnp.concatenate([ckvn, r1, r2], axis=-1)


def _mla_q_kernel(c_ref, wn_ref, w1_ref, w2_ref, wk_ref, cos_ref, sin_ref, o_ref, *, kl, half):
    c = c_ref[...]
    q_nope = _dot(c, wn_ref[...].astype(MXU_DTYPE))
    x1 = _dot(c, w1_ref[...].astype(MXU_DTYPE))
    x2 = _dot(c, w2_ref[...].astype(MXU_DTYPE))
    cos, sin = cos_ref[...], sin_ref[...]
    o_ref[:, 0:kl] = _nt_dot(q_nope.astype(MXU_DTYPE), wk_ref[...].astype(MXU_DTYPE))
    o_ref[:, kl:kl + half] = x1 * cos - x2 * sin
    o_ref[:, kl + half:kl + 2 * half] = x1 * sin + x2 * cos


def mla_queries(cqn, w_uq, w_uk, cos, sin, *, tm=512):
    m, ql = cqn.shape
    kl, n_heads, nope = w_uk.shape
    half = cos.shape[1]
    tm = _tile(m, tm)
    w_heads = jnp.transpose(w_uq, (1, 0, 2))
    wk_heads = jnp.transpose(w_uk, (1, 0, 2))
    return pl.pallas_call(
        functools.partial(_mla_q_kernel, kl=kl, half=half),
        grid=(m // tm, n_heads),
        in_specs=[pl.BlockSpec((tm, ql), lambda i, h: (i, 0)),
                  pl.BlockSpec((None, ql, nope), lambda i, h: (h, 0, 0)),
                  pl.BlockSpec((None, ql, half), lambda i, h: (h, 0, 0)),
                  pl.BlockSpec((None, ql, half), lambda i, h: (h, 0, 0)),
                  pl.BlockSpec((None, kl, nope), lambda i, h: (h, 0, 0)),
                  pl.BlockSpec((tm, half), lambda i, h: (i, 0)),
                  pl.BlockSpec((tm, half), lambda i, h: (i, 0))],
        out_specs=pl.BlockSpec((None, tm, kl + 2 * half), lambda i, h: (h, i, 0)),
        out_shape=jax.ShapeDtypeStruct((n_heads, m, kl + 2 * half), F32),
        compiler_params=_params(("parallel", "parallel"), 32 << 20),
    )(cqn, w_heads[:, :, :nope], w_heads[:, :, nope:nope + half], w_heads[:, :, nope + half:], wk_heads, cos, sin)


def _fox_gate_kernel(x_ref, w_ref, b_ref, o_ref):
    z = _dot(x_ref[...], w_ref[...].astype(MXU_DTYPE)) + b_ref[...]
    o_ref[...] = jnp.minimum(z, 0.0) - jnp.log1p(jnp.exp(-jnp.abs(z)))


def fox_log_forget(xb, w_f, b_f, *, tm=1024):
    m, d = xb.shape
    h = w_f.shape[1]
    tm = _tile(m, tm)
    return pl.pallas_call(
        _fox_gate_kernel,
        grid=(m // tm,),
        in_specs=[pl.BlockSpec((tm, d), lambda i: (i, 0)),
                  pl.BlockSpec((d, h), lambda i: (0, 0)),
                  pl.BlockSpec((1, h), lambda i: (0, 0))],
        out_specs=pl.BlockSpec((tm, h), lambda i: (i, 0)),
        out_shape=jax.ShapeDtypeStruct((m, h), F32),
        compiler_params=_params(("parallel",), 2 * (tm * d * 2 + d * LANES * 4 + tm * LANES * 4) + (8 << 20)),
    )(xb, w_f, b_f.reshape(1, h))


def _softmax_init(m_ref, l_ref, acc_ref):
    m_ref[...] = jnp.full(m_ref.shape, MASK_VALUE, F32)
    l_ref[...] = jnp.zeros(l_ref.shape, F32)
    acc_ref[...] = jnp.zeros(acc_ref.shape, F32)


def _softmax_step(s, v, m_ref, l_ref, acc_ref):
    m_old = m_ref[...]
    m_new = jnp.maximum(m_old, jnp.max(s, axis=1, keepdims=True))
    alpha = jnp.exp(m_old - m_new)
    p = jnp.exp(s - m_new)
    l_ref[...] = alpha * l_ref[...] + jnp.sum(p, axis=1, keepdims=True)
    acc_ref[...] = alpha * acc_ref[...] + _dot(p.astype(MXU_DTYPE), v)
    m_ref[...] = m_new


def _heads_to_rows(q, n_heads, hd):
    return jnp.concatenate([q[:, h * hd:(h + 1) * hd] for h in range(n_heads)], axis=0)


def _rows_to_heads(o_ref, o, n_heads, t, hd):
    for h in range(n_heads):
        o_ref[:, h * hd:(h + 1) * hd] = o[h * t:(h + 1) * t, :].astype(o_ref.dtype)


def _prefix_sum_rows(x, carry):
    n = x.shape[0]
    tri = (lax.broadcasted_iota(jnp.int32, (n, n), 1) <= lax.broadcasted_iota(jnp.int32, (n, n), 0)).astype(MXU_DTYPE)
    hi = x.astype(MXU_DTYPE)
    r1 = x - hi.astype(F32)
    mid = r1.astype(MXU_DTYPE)
    lo = (r1 - mid.astype(F32)).astype(MXU_DTYPE)
    return _dot(tri, hi) + _dot(tri, mid) + _dot(tri, lo) + carry


def _order_key(score):
    bits = lax.bitcast_convert_type(score, jnp.int32)
    return jnp.where(bits < 0, bits ^ jnp.int32(0x7FFFFFFF), bits)


def _kth_largest_key(key, k):
    r = key.shape[0]

    def count_ge(t):
        return jnp.sum(jnp.where(key >= t, 1.0, 0.0), axis=1, keepdims=True)

    int_min = jnp.int32(-2 ** 31)
    thr = jnp.where(count_ge(jnp.zeros((r, 1), jnp.int32)) >= k, jnp.int32(0), int_min)

    def body(it, thr):
        cand = thr | lax.shift_left(jnp.int32(1), jnp.int32(30) - it)
        return jnp.where(count_ge(cand) >= k, cand, thr)

    return lax.fori_loop(0, 31, body, thr)


def bias_tables(rel_bias, tq):
    max_exact = N_BUCKETS // 2
    d = np.arange(0, 2 * LANES + tq)
    ratio = np.log(np.maximum(d, 1).astype(np.float32) / max_exact) / math.log(MAX_DISTANCE / max_exact)
    large = np.minimum(max_exact + (ratio * (N_BUCKETS - max_exact)).astype(np.int32), N_BUCKETS - 1)
    bucket = np.where(d < max_exact, d, large)
    i = np.arange(tq)[:, None]
    j = np.arange(LANES)[None, :]
    dist = np.stack([np.maximum(i - j, 0), i - j + LANES, np.full((tq, LANES), 2 * LANES)])
    return jnp.transpose(rel_bias[bucket[dist]], (0, 3, 1, 2)).astype(F32)


def _mla_prompt_kernel(q_ref, k_ref, wuv_ref, o_ref, m_ref, l_ref, acc_ref, *, tq, tk, scale, dv):
    i = pl.program_id(1)
    n_heads, _, dk = q_ref.shape
    rows = n_heads * tq
    q = q_ref[...].reshape(rows, dk).astype(MXU_DTYPE)
    row_pos = i * tq + lax.broadcasted_iota(jnp.int32, (rows, 1), 0) % tq
    _softmax_init(m_ref, l_ref, acc_ref)

    def step(kb, _):
        k = k_ref[pl.ds(pl.multiple_of(kb * tk, tk), tk), :].astype(MXU_DTYPE)
        s = _nt_dot(q, k) * scale
        k_pos = kb * tk + lax.broadcasted_iota(jnp.int32, (1, tk), 1)
        s = jnp.where(k_pos <= row_pos, s, MASK_VALUE)
        _softmax_step(s, k[:, :dv], m_ref, l_ref, acc_ref)
        return 0

    lax.fori_loop(0, (i * tq + tq + tk - 1) // tk, step, 0)
    o_lat = (acc_ref[...] / l_ref[...]).astype(MXU_DTYPE)
    hd = wuv_ref.shape[2]
    for h in range(n_heads):
        o_ref[:, h * hd:(h + 1) * hd] = _dot(o_lat[h * tq:(h + 1) * tq, :],
                                             wuv_ref[h].astype(MXU_DTYPE)).astype(o_ref.dtype)


def mla_prompt(q_cat, rows, w_uv_heads, n_seq, t, scale, *, tq=128, tk=256):
    n_heads, _, dk = q_cat.shape
    _, dv, hd = w_uv_heads.shape
    tq, tk = _tile(t, tq), _tile(t, tk)
    nq = t // tq
    r = n_heads * tq
    vmem = (2 * (r * dk * 4 + t * dk * 4 + n_heads * dv * hd * 4 + tq * n_heads * hd * 2)
            + r * dk * 2 + 2 * r * LANES * 4 + 2 * r * dv * 4 + 4 * r * tk * 4 + (6 << 20))
    return pl.pallas_call(
        functools.partial(_mla_prompt_kernel, tq=tq, tk=tk, scale=scale, dv=dv),
        grid=(n_seq, nq),
        in_specs=[pl.BlockSpec((n_heads, tq, dk), lambda b, i: (0, b * nq + i, 0)),
                  pl.BlockSpec((t, dk), lambda b, i: (b, 0)),
                  pl.BlockSpec((n_heads, dv, hd), lambda b, i: (0, 0, 0))],
        out_specs=pl.BlockSpec((tq, n_heads * hd), lambda b, i: (b * nq + i, 0)),
        out_shape=jax.ShapeDtypeStruct((n_seq * t, n_heads * hd), MXU_DTYPE),
        scratch_shapes=[pltpu.VMEM((r, 1), F32), pltpu.VMEM((r, 1), F32), pltpu.VMEM((r, dv), F32)],
        compiler_params=_params(("parallel", "arbitrary"), vmem),
    )(q_cat, rows, w_uv_heads)


def _fox_prompt_kernel(q_ref, kv_ref, cum_ref, cumt_ref, o_ref, m_ref, l_ref, acc_ref, *, tq, tk, scale, hd):
    i = pl.program_id(1)
    n_heads = q_ref.shape[1] // hd
    rows = n_heads * tq
    q = _heads_to_rows(q_ref[...].astype(MXU_DTYPE), n_heads, hd)
    cum_q = cum_ref[...]
    cum_q = jnp.stack([cum_q[:, h:h + 1] for h in range(n_heads)])
    row_pos = i * tq + lax.broadcasted_iota(jnp.int32, (rows, 1), 0) % tq
    _softmax_init(m_ref, l_ref, acc_ref)

    def step(kb, _):
        start = pl.multiple_of(kb * tk, tk)
        kv = kv_ref[pl.ds(start, tk), :].astype(MXU_DTYPE)
        cum_k = cumt_ref[:, pl.ds(start, tk)]
        s = _nt_dot(q, kv[:, :hd]).reshape(n_heads, tq, tk) * scale + cum_q - cum_k[:, None, :]
        k_pos = kb * tk + lax.broadcasted_iota(jnp.int32, (1, tk), 1)
        s = jnp.where(k_pos <= row_pos, s.reshape(rows, tk), MASK_VALUE)
        _softmax_step(s, kv[:, hd:], m_ref, l_ref, acc_ref)
        return 0

    lax.fori_loop(0, (i * tq + tq + tk - 1) // tk, step, 0)
    _rows_to_heads(o_ref, acc_ref[...] / l_ref[...], n_heads, tq, hd)


def fox_prompt(q, kv_rows, cum, cum_t, n_seq, t, scale, *, tq=128, tk=256):
    width = q.shape[1]
    hd = kv_rows.shape[1] // 2
    n_heads = width // hd
    tq, tk = _tile(t, tq), _tile(t, tk)
    nq = t // tq
    r = n_heads * tq
    vmem = (2 * (2 * tq * width * 2 + t * 2 * hd * 4 + tq * LANES * 4 + n_heads * t * 4)
            + r * hd * 2 + 2 * r * LANES * 4 + 2 * r * hd * 4 + 5 * r * tk * 4 + (6 << 20))
    return pl.pallas_call(
        functools.partial(_fox_prompt_kernel, tq=tq, tk=tk, scale=scale, hd=hd),
        grid=(n_seq, nq),
        in_specs=[pl.BlockSpec((tq, width), lambda b, i: (b * nq + i, 0)),
                  pl.BlockSpec((t, 2 * hd), lambda b, i: (b, 0)),
                  pl.BlockSpec((tq, n_heads), lambda b, i: (b * nq + i, 0)),
                  pl.BlockSpec((None, n_heads, t), lambda b, i: (b, 0, 0))],
        out_specs=pl.BlockSpec((tq, width), lambda b, i: (b * nq + i, 0)),
        out_shape=jax.ShapeDtypeStruct((n_seq * t, width), MXU_DTYPE),
        scratch_shapes=[pltpu.VMEM((r, 1), F32), pltpu.VMEM((r, 1), F32), pltpu.VMEM((r, hd), F32)],
        compiler_params=_params(("parallel", "arbitrary"), vmem),
    )(q, kv_rows, cum, cum_t)


def _cumsum_prompt_kernel(x_ref, o_ref, *, blk):
    t, c = x_ref.shape
    carry = jnp.zeros((1, c), F32)
    for s in range(0, t, blk):
        out = _prefix_sum_rows(x_ref[s:s + blk, :], carry)
        o_ref[s:s + blk, :] = out
        carry = out[blk - 1:blk, :]


def cumsum_prompt(logf, n_seq, t):
    c = logf.shape[1]
    blk = _tile(t, LANES)
    return pl.pallas_call(
        functools.partial(_cumsum_prompt_kernel, blk=blk),
        grid=(n_seq,),
        in_specs=[pl.BlockSpec((t, c), lambda b: (b, 0))],
        out_specs=pl.BlockSpec((t, c), lambda b: (b, 0)),
        out_shape=jax.ShapeDtypeStruct((n_seq * t, c), F32),
        compiler_params=_params(("parallel",), 16 << 20),
    )(logf)


def _dsa_prompt_kernel(qa_ref, qi_ref, wi_ref, rows_ref, bias_ref, o_ref,
                       score_ref, m_ref, l_ref, acc_ref, *, tq, tk, topk, scale, w_scale, hd, idx_dim):
    i = pl.program_id(1)
    t = rows_ref.shape[0]
    n_heads = qa_ref.shape[1] // hd
    n_idx = qi_ref.shape[1] // idx_dim
    q_pos = i * tq + lax.broadcasted_iota(jnp.int32, (tq, 1), 0)

    score_ref[...] = jnp.full(score_ref.shape, -jnp.inf, F32)
    qi = qi_ref[...].astype(MXU_DTYPE)
    wi = wi_ref[...] * w_scale

    def score_step(kc, _):
        start = pl.multiple_of(kc * tk, tk)
        ki = rows_ref[pl.ds(start, tk), 2 * hd:2 * hd + idx_dim].astype(MXU_DTYPE)
        acc = jnp.zeros((tq, tk), F32)
        for h in range(n_idx):
            act = jnp.maximum(_nt_dot(qi[:, h * idx_dim:(h + 1) * idx_dim], ki), 0.0)
            acc = acc + wi[:, h:h + 1] * act
        k_pos = kc * tk + lax.broadcasted_iota(jnp.int32, (1, tk), 1)
        score_ref[:, pl.ds(start, tk)] = jnp.where(k_pos <= q_pos, acc, -jnp.inf)
        return 0

    lax.fori_loop(0, (i * tq + tq + tk - 1) // tk, score_step, 0)

    score = score_ref[...]
    key = _order_key(score)
    thr = _kth_largest_key(key, topk)
    score_ref[...] = jnp.where((key >= thr) & (score > -jnp.inf), 0.0, MASK_VALUE)

    q = _heads_to_rows(qa_ref[...].astype(MXU_DTYPE), n_heads, hd)
    _softmax_init(m_ref, l_ref, acc_ref)

    def attn_step(kb, _):
        start = pl.multiple_of(kb * LANES, LANES)
        kv = rows_ref[pl.ds(start, LANES), 0:2 * hd].astype(MXU_DTYPE)
        bias = bias_ref[jnp.minimum(i * (tq // LANES) - kb, 2)]
        keep = score_ref[:, pl.ds(start, LANES)]
        s = _nt_dot(q, kv[:, :hd]).reshape(n_heads, tq, LANES) * scale + bias + keep[None]
        _softmax_step(s.reshape(n_heads * tq, LANES), kv[:, hd:], m_ref, l_ref, acc_ref)
        return 0

    lax.fori_loop(0, (i * tq + tq) // LANES, attn_step, 0)
    _rows_to_heads(o_ref, acc_ref[...] / l_ref[...], n_heads, tq, hd)


def dsa_prompt(q_a, q_i, w_i, rows, bias, n_seq, t, scale, w_scale, hd, idx_dim):
    tq = LANES
    assert t % tq == 0
    tk = _tile(t, 256, LANES)
    nq = t // tq
    n_heads = q_a.shape[1] // hd
    r = n_heads * tq
    row_w = rows.shape[1]
    vmem = (2 * (2 * tq * q_a.shape[1] * 2 + tq * q_i.shape[1] * 2 + tq * LANES * 4 + t * row_w * 4 + bias.size * 4)
            + 4 * tq * t * 4 + 2 * r * LANES * 4 + 2 * r * hd * 4 + 6 * r * LANES * 4 + (6 << 20))
    return pl.pallas_call(
        functools.partial(_dsa_prompt_kernel, tq=tq, tk=tk, topk=min(DSA_TOPK, t // 4), scale=scale,
                          w_scale=w_scale, hd=hd, idx_dim=idx_dim),
        grid=(n_seq, nq),
        in_specs=[pl.BlockSpec((tq, q_a.shape[1]), lambda b, i: (b * nq + i, 0)),
                  pl.BlockSpec((tq, q_i.shape[1]), lambda b, i: (b * nq + i, 0)),
                  pl.BlockSpec((tq, w_i.shape[1]), lambda b, i: (b * nq + i, 0)),
                  pl.BlockSpec((t, row_w), lambda b, i: (b, 0)),
                  pl.BlockSpec(bias.shape, lambda b, i: (0, 0, 0, 0))],
        out_specs=pl.BlockSpec((tq, q_a.shape[1]), lambda b, i: (b * nq + i, 0)),
        out_shape=jax.ShapeDtypeStruct((n_seq * t, q_a.shape[1]), MXU_DTYPE),
        scratch_shapes=[pltpu.VMEM((tq, t), F32), pltpu.VMEM((r, 1), F32), pltpu.VMEM((r, 1), F32),
                        pltpu.VMEM((r, hd), F32)],
        compiler_params=_params(("parallel", "arbitrary"), vmem),
    )(q_a, q_i, w_i, rows, bias)


def _page_specs(pool, layer, n_group):
    _, _, page, width = pool.shape

    def spec(p):
        return pl.BlockSpec((None, None, page, width), lambda b, g, pt: (layer, pt[b, g * n_group + p], 0, 0))

    return [spec(p) for p in range(n_group)]


def _pad_rows(x, n):
    return jnp.concatenate([x, jnp.zeros((n - x.shape[0], x.shape[1]), x.dtype)], axis=0)


def _mla_sample_kernel(pt_ref, q_ref, new_ref, wuv_ref, *rest, n_group, scale, dv):
    pages = rest[:n_group]
    o_ref, m_ref, l_ref, acc_ref = rest[n_group:]
    g = pl.program_id(1)
    n_heads, tq, dk = q_ref.shape
    rows = n_heads * tq
    q = q_ref[...].reshape(rows, dk).astype(MXU_DTYPE)

    @pl.when(g == 0)
    def _():
        _softmax_init(m_ref, l_ref, acc_ref)

    for page in pages:
        k = page[...].astype(MXU_DTYPE)
        _softmax_step(_nt_dot(q, k) * scale, k[:, :dv], m_ref, l_ref, acc_ref)

    @pl.when(g == pl.num_programs(1) - 1)
    def _():
        k = _pad_rows(new_ref[...], LANES).astype(MXU_DTYPE)
        t_row = lax.broadcasted_iota(jnp.int32, (rows, 1), 0) % tq
        j = lax.broadcasted_iota(jnp.int32, (1, LANES), 1)
        s = jnp.where(j <= t_row, _nt_dot(q, k) * scale, MASK_VALUE)
        _softmax_step(s, k[:, :dv], m_ref, l_ref, acc_ref)
        o_lat = (acc_ref[...] / l_ref[...]).astype(MXU_DTYPE)
        hd = wuv_ref.shape[2]
        for h in range(n_heads):
            o_ref[:, h * hd:(h + 1) * hd] = _dot(o_lat[h * tq:(h + 1) * tq, :], wuv_ref[h].astype(MXU_DTYPE))


def mla_sample(q_cat, rows, pool, layer, page_table, w_uv_heads, row0, tq, scale, *, n_group=16):
    n_heads, _, dk = q_cat.shape
    _, dv, hd = w_uv_heads.shape
    n_seq, n_pages = page_table.shape
    n_group = _tile(n_pages, n_group, 1)
    qb0 = row0 // tq
    r = n_heads * tq
    grid_spec = pltpu.PrefetchScalarGridSpec(
        num_scalar_prefetch=1,
        grid=(n_seq, n_pages // n_group),
        in_specs=[pl.BlockSpec((n_heads, tq, dk), lambda b, g, pt: (0, qb0 + b, 0)),
                  pl.BlockSpec((tq, dk), lambda b, g, pt: (qb0 + b, 0)),
                  pl.BlockSpec((n_heads, dv, hd), lambda b, g, pt: (0, 0, 0))] + _page_specs(pool, layer, n_group),
        out_specs=pl.BlockSpec((tq, n_heads * hd), lambda b, g, pt: (b, 0)),
        scratch_shapes=[pltpu.VMEM((r, 1), F32), pltpu.VMEM((r, 1), F32), pltpu.VMEM((r, dv), F32)],
    )
    vmem = 2 * (n_group * pool.shape[2] * pool.shape[3] * 4 + n_heads * dv * hd * 4 + r * dk * 4) + (16 << 20)
    return pl.pallas_call(
        functools.partial(_mla_sample_kernel, n_group=n_group, scale=scale, dv=dv),
        grid_spec=grid_spec,
        out_shape=jax.ShapeDtypeStruct((n_seq * tq, n_heads * hd), F32),
        compiler_params=_params(("parallel", "arbitrary"), vmem),
    )(page_table, q_cat, rows, w_uv_heads, *([pool] * n_group))


def _cumsum_sample_kernel(pt_ref, new_ref, *rest, n_group):
    pages = rest[:n_group]
    cum_ref, cum_new_ref, carry_ref = rest[n_group:]
    g = pl.program_id(1)
    page = pages[0].shape[0]

    @pl.when(g == 0)
    def _():
        carry_ref[...] = jnp.zeros(carry_ref.shape, F32)

    carry = carry_ref[...]
    for p, ref in enumerate(pages):
        out = _prefix_sum_rows(ref[...], carry)
        cum_ref[p * page:(p + 1) * page, :] = out
        carry = out[page - 1:page, :]
    carry_ref[...] = carry

    @pl.when(g == pl.num_programs(1) - 1)
    def _():
        tq = new_ref.shape[0]
        cum_new_ref[...] = _prefix_sum_rows(_pad_rows(new_ref[...], LANES), carry)[:tq, :]


def cumsum_sample(logf_new, pool, layer, page_table, row0, tq, *, n_group=16):
    n_seq, n_pages = page_table.shape
    page, c = pool.shape[2], pool.shape[3]
    n_group = _tile(n_pages, n_group, 1)
    qb0 = row0 // tq
    grid_spec = pltpu.PrefetchScalarGridSpec(
        num_scalar_prefetch=1,
        grid=(n_seq, n_pages // n_group),
        in_specs=[pl.BlockSpec((tq, c), lambda b, g, pt: (qb0 + b, 0))] + _page_specs(pool, layer, n_group),
        out_specs=[pl.BlockSpec((None, n_group * page, c), lambda b, g, pt: (b, g, 0)),
                   pl.BlockSpec((None, tq, c), lambda b, g, pt: (b, 0, 0))],
        scratch_shapes=[pltpu.VMEM((1, c), F32)],
    )
    return pl.pallas_call(
        functools.partial(_cumsum_sample_kernel, n_group=n_group),
        grid_spec=grid_spec,
        out_shape=[jax.ShapeDtypeStruct((n_seq, n_pages * page, c), F32),
                   jax.ShapeDtypeStruct((n_seq, tq, c), F32)],
        compiler_params=_params(("parallel", "arbitrary"), 32 << 20),
    )(page_table, logf_new, *([pool] * n_group))


def _fox_sample_kernel(pt_ref, q_ref, new_ref, cumq_ref, cumt_ref, cumt_new_ref, *rest, n_group, scale, hd):
    pages = rest[:n_group]
    o_ref, m_ref, l_ref, acc_ref = rest[n_group:]
    g = pl.program_id(1)
    tq = q_ref.shape[0]
    n_heads = q_ref.shape[1] // hd
    rows = n_heads * tq
    page = pages[0].shape[0]
    q = _heads_to_rows(q_ref[...].astype(MXU_DTYPE), n_heads, hd)
    cum_q = cumq_ref[...]
    cum_q = jnp.stack([cum_q[:, h:h + 1] for h in range(n_heads)])

    @pl.when(g == 0)
    def _():
        _softmax_init(m_ref, l_ref, acc_ref)

    for p, ref in enumerate(pages):
        kv = ref[...].astype(MXU_DTYPE)
        cum_k = cumt_ref[:, p * page:(p + 1) * page]
        s = _nt_dot(q, kv[:, :hd]).reshape(n_heads, tq, page) * scale + cum_q - cum_k[:, None, :]
        _softmax_step(s.reshape(rows, page), kv[:, hd:], m_ref, l_ref, acc_ref)

    @pl.when(g == pl.num_programs(1) - 1)
    def _():
        kv = _pad_rows(new_ref[...], LANES).astype(MXU_DTYPE)
        cum_k = cumt_new_ref[...]
        s = _nt_dot(q, kv[:, :hd]).reshape(n_heads, tq, LANES) * scale + cum_q - cum_k[:, None, :]
        t_row = lax.broadcasted_iota(jnp.int32, (rows, 1), 0) % tq
        j = lax.broadcasted_iota(jnp.int32, (1, LANES), 1)
        s = jnp.where(j <= t_row, s.reshape(rows, LANES), MASK_VALUE)
        _softmax_step(s, kv[:, hd:], m_ref, l_ref, acc_ref)
        _rows_to_heads(o_ref, acc_ref[...] / l_ref[...], n_heads, tq, hd)


def fox_sample(q, kv_new, cum_new, cum_t, cum_t_new, pool, layer, page_table, row0, tq, scale, *, n_group=16):
    width = q.shape[1]
    hd = kv_new.shape[1] // 2
    n_heads = width // hd
    n_seq, n_pages = page_table.shape
    page = pool.shape[2]
    n_group = _tile(n_pages, n_group, 1)
    qb0 = row0 // tq
    r = n_heads * tq
    grid_spec = pltpu.PrefetchScalarGridSpec(
        num_scalar_prefetch=1,
        grid=(n_seq, n_pages // n_group),
        in_specs=[pl.BlockSpec((tq, width), lambda b, g, pt: (qb0 + b, 0)),
                  pl.BlockSpec((tq, 2 * hd), lambda b, g, pt: (qb0 + b, 0)),
                  pl.BlockSpec((None, tq, n_heads), lambda b, g, pt: (b, 0, 0)),
                  pl.BlockSpec((None, n_heads, n_group * page), lambda b, g, pt: (b, 0, g)),
                  pl.BlockSpec((None, n_heads, LANES), lambda b, g, pt: (b, 0, 0))] + _page_specs(pool, layer, n_group),
        out_specs=pl.BlockSpec((tq, width), lambda b, g, pt: (b, 0)),
        scratch_shapes=[pltpu.VMEM((r, 1), F32), pltpu.VMEM((r, 1), F32), pltpu.VMEM((r, hd), F32)],
    )
    return pl.pallas_call(
        functools.partial(_fox_sample_kernel, n_group=n_group, scale=scale, hd=hd),
        grid_spec=grid_spec,
        out_shape=jax.ShapeDtypeStruct((n_seq * tq, width), F32),
        compiler_params=_params(("parallel", "arbitrary"), 40 << 20),
    )(page_table, q, kv_new, cum_new, cum_t, cum_t_new, *([pool] * n_group))


def _idx_scores(qi_rows, wi, ki, n_idx, tq):
    z = jnp.maximum(_nt_dot(qi_rows, ki), 0.0)
    acc = jnp.zeros((tq, ki.shape[0]), F32)
    for h in range(n_idx):
        acc = acc + wi[:, h:h + 1] * z[h * tq:(h + 1) * tq, :]
    return acc


def _dsa_score_sample_kernel(pt_ref, qi_ref, wi_ref, new_ref, *rest, n_group, w_scale, hd, idx_dim):
    pages = rest[:n_group]
    past_ref, new_score_ref = rest[n_group:]
    tq = qi_ref.shape[0]
    n_idx = qi_ref.shape[1] // idx_dim
    page = pages[0].shape[0]
    qi = _heads_to_rows(qi_ref[...].astype(MXU_DTYPE), n_idx, idx_dim)
    wi = wi_ref[...] * w_scale
    for p, ref in enumerate(pages):
        ki = ref[:, 2 * hd:2 * hd + idx_dim].astype(MXU_DTYPE)
        past_ref[:, p * page:(p + 1) * page] = _idx_scores(qi, wi, ki, n_idx, tq)

    @pl.when(pl.program_id(1) == pl.num_programs(1) - 1)
    def _():
        ki = _pad_rows(new_ref[:, 2 * hd:2 * hd + idx_dim], LANES).astype(MXU_DTYPE)
        t_row = lax.broadcasted_iota(jnp.int32, (tq, 1), 0)
        j = lax.broadcasted_iota(jnp.int32, (1, LANES), 1)
        new_score_ref[...] = jnp.where(j <= t_row, _idx_scores(qi, wi, ki, n_idx, tq), -jnp.inf)


def dsa_score_sample(q_i, w_i, rows, pool, layer, page_table, row0, tq, w_scale, hd, idx_dim, *, n_group=16):
    n_seq, n_pages = page_table.shape
    page = pool.shape[2]
    n_group = _tile(n_pages, n_group, 1)
    qb0 = row0 // tq
    grid_spec = pltpu.PrefetchScalarGridSpec(
        num_scalar_prefetch=1,
        grid=(n_seq, n_pages // n_group),
        in_specs=[pl.BlockSpec((tq, q_i.shape[1]), lambda b, g, pt: (qb0 + b, 0)),
                  pl.BlockSpec((tq, w_i.shape[1]), lambda b, g, pt: (qb0 + b, 0)),
                  pl.BlockSpec((tq, rows.shape[1]), lambda b, g, pt: (qb0 + b, 0))] + _page_specs(pool, layer, n_group),
        out_specs=[pl.BlockSpec((None, tq, n_group * page), lambda b, g, pt: (b, 0, g)),
                   pl.BlockSpec((None, tq, LANES), lambda b, g, pt: (b, 0, 0))],
    )
    past, new = pl.pallas_call(
        functools.partial(_dsa_score_sample_kernel, n_group=n_group, w_scale=w_scale, hd=hd, idx_dim=idx_dim),
        grid_spec=grid_spec,
        out_shape=[jax.ShapeDtypeStruct((n_seq, tq, n_pages * page), F32),
                   jax.ShapeDtypeStruct((n_seq, tq, LANES), F32)],
        compiler_params=_params(("parallel", "arbitrary"), 40 << 20),
    )(page_table, q_i, w_i, rows, *([pool] * n_group))
    return jnp.concatenate([past, new], axis=-1)


def _dsa_attn_sample_kernel(pt_ref, qa_ref, score_ref, new_ref, bias_ref, *rest, n_group, n_blocks, topk, scale, hd):
    pages = rest[:n_group]
    o_ref, keep_ref, m_ref, l_ref, acc_ref = rest[n_group:]
    g = pl.program_id(1)
    tq = qa_ref.shape[0]
    n_heads = qa_ref.shape[1] // hd
    rows = n_heads * tq
    page = pages[0].shape[0]
    q = _heads_to_rows(qa_ref[...].astype(MXU_DTYPE), n_heads, hd)

    @pl.when(g == 0)
    def _():
        _softmax_init(m_ref, l_ref, acc_ref)
        score = score_ref[...]
        key = _order_key(score)
        thr = _kth_largest_key(key, topk)
        keep_ref[...] = jnp.where((key >= thr) & (score > -jnp.inf), 0.0, MASK_VALUE)

    def attend(kv, kb):
        bias = bias_ref[jnp.minimum(n_blocks - kb, 2)]
        keep = keep_ref[:, pl.ds(pl.multiple_of(kb * page, page), page)]
        s = _nt_dot(q, kv[:, :hd]).reshape(n_heads, tq, page) * scale + bias + keep[None]
        _softmax_step(s.reshape(rows, page), kv[:, hd:], m_ref, l_ref, acc_ref)

    for p, ref in enumerate(pages):
        attend(ref[:, 0:2 * hd].astype(MXU_DTYPE), g * n_group + p)

    @pl.when(g == pl.num_programs(1) - 1)
    def _():
        attend(_pad_rows(new_ref[:, 0:2 * hd], page).astype(MXU_DTYPE), n_blocks)
        _rows_to_heads(o_ref, acc_ref[...] / l_ref[...], n_heads, tq, hd)


def dsa_attn_sample(q_a, score, rows, bias, pool, layer, page_table, row0, tq, scale, hd, *, n_group=16):
    n_seq, n_pages = page_table.shape
    page = pool.shape[2]
    assert page == LANES
    n_group = _tile(n_pages, n_group, 1)
    qb0 = row0 // tq
    width = q_a.shape[1]
    n_keys = n_pages * page + tq
    r = (width // hd) * tq
    grid_spec = pltpu.PrefetchScalarGridSpec(
        num_scalar_prefetch=1,
        grid=(n_seq, n_pages // n_group),
        in_specs=[pl.BlockSpec((tq, width), lambda b, g, pt: (qb0 + b, 0)),
                  pl.BlockSpec((None, tq, score.shape[2]), lambda b, g, pt: (b, 0, 0)),
                  pl.BlockSpec((tq, rows.shape[1]), lambda b, g, pt: (qb0 + b, 0)),
                  pl.BlockSpec(bias.shape, lambda b, g, pt: (0, 0, 0, 0))] + _page_specs(pool, layer, n_group),
        out_specs=pl.BlockSpec((tq, width), lambda b, g, pt: (b, 0)),
        scratch_shapes=[pltpu.VMEM((tq, score.shape[2]), F32), pltpu.VMEM((r, 1), F32), pltpu.VMEM((r, 1), F32),
                        pltpu.VMEM((r, hd), F32)],
    )
    return pl.pallas_call(
        functools.partial(_dsa_attn_sample_kernel, n_group=n_group, n_blocks=n_pages, topk=min(DSA_TOPK, n_keys // 4),
                          scale=scale, hd=hd),
        grid_spec=grid_spec,
        out_shape=jax.ShapeDtypeStruct((n_seq * tq, width), F32),
        compiler_params=_params(("parallel", "arbitrary"), 40 << 20),
    )(page_table, q_a, score, rows, bias, *([pool] * n_group))


def kernel(x_prompt, x_sample, cache_dsa, cache_mla, cache_fox_kv, cache_fox_logf, page_table, p_prompt, p_sample,
           rel_bias, w_in_ab, mla_q_norm, w_uq, mla_kv_norm, w_uk, w_uv, w_o_ab, w_in_c, b_forget, w_o_c,
           ln1_g, ln1_b, ln2_g, ln2_b, w_ffn_up, w_ffn_down, w_router, w_moe_up, w_moe_down,
           w_ple_gate, b_ple_gate, w_ple_proj):
    n_p, t_p, d = x_prompt.shape
    n_s, t_s, _ = x_sample.shape
    depth = ln1_g.shape[0]
    m_p, m_s = n_p * t_p, n_s * t_s
    hd = cache_fox_kv.shape[-1] // 2
    idx_dim = cache_dsa.shape[-1] - 2 * hd
    kv_lora = w_uk.shape[1]
    rope = cache_mla.shape[-1] - kv_lora
    q_lora = w_uq.shape[1]
    nope = w_uk.shape[3]
    h_a = rel_bias.shape[1]
    h_c = b_forget.shape[1]
    ab_cols = w_in_ab.shape[2]
    n_idx_w = (ab_cols - h_a * hd - 2 * hd - idx_dim - q_lora - kv_lora - rope) // (idx_dim + 1)
    past_len = page_table.shape[1] * cache_dsa.shape[2]
    alpha = (2 * depth) ** 0.25
    attn_scale = hd ** -0.5
    mla_scale = (nope + rope) ** -0.5
    idx_w_scale = (n_idx_w * idx_dim) ** -0.5

    c_qa = 0
    c_kv = h_a * hd
    c_qi = c_kv + 2 * hd
    c_tail = c_qi + n_idx_w * idx_dim
    t_wi = idx_dim
    t_cq = t_wi + n_idx_w
    t_ckv = t_cq + q_lora
    t_kr = t_ckv + kv_lora

    x = jnp.concatenate([x_prompt.reshape(m_p, d), x_sample.reshape(m_s, d)], axis=0)
    xb = x.astype(MXU_DTYPE)
    ple = jnp.concatenate([p_prompt.reshape(depth, m_p, -1), p_sample.reshape(depth, m_s, -1)], axis=1)
    pos = jnp.concatenate([jnp.tile(jnp.arange(t_p, dtype=jnp.int32), n_p),
                           past_len + jnp.tile(jnp.arange(t_s, dtype=jnp.int32), n_s)])
    inv = ROPE_THETA ** (-jnp.arange(0, rope, 2, dtype=F32) / rope)
    ang = pos.astype(F32)[:, None] * inv[None, :]
    cos, sin = jnp.cos(ang), jnp.sin(ang)
    bias_p = bias_tables(rel_bias, LANES)
    bias_s = bias_tables(rel_bias, t_s)

    new_dsa, new_mla, new_fkv, new_flf = [], [], [], []
    for i in range(depth):
        j = i // 2
        if i % 2 == 0:
            w_in = w_in_ab[j]
            q_a = matmul(xb, w_in, out_dtype=F32, col_start=c_qa, n_cols=h_a * hd)
            kv_a = matmul(xb, w_in, out_dtype=F32, col_start=c_kv, n_cols=2 * hd)
            q_i = matmul(xb, w_in, out_dtype=F32, col_start=c_qi, n_cols=n_idx_w * idx_dim)
            tail = matmul(xb, w_in[:, c_tail:], out_dtype=F32)
            w_i = tail[:, t_wi:t_cq]
            cqn, mla_rows = ab_post(tail[:, t_cq:t_ckv], tail[:, t_ckv:t_kr], tail[:, t_kr:], cos, sin,
                                    mla_q_norm[j], mla_kv_norm[j])
            dsa_rows = jnp.concatenate([kv_a, tail[:, :t_wi]], axis=-1)
            q_cat = mla_queries(cqn, w_uq[j], w_uk[j], cos, sin)
            w_uv_heads = jnp.transpose(w_uv[j], (1, 0, 2))

            o_a_p = dsa_prompt(q_a, q_i, w_i, dsa_rows, bias_p, n_p, t_p, attn_scale, idx_w_scale, hd, idx_dim)
            score_s = dsa_score_sample(q_i, w_i, dsa_rows, cache_dsa, j, page_table, m_p, t_s, idx_w_scale, hd, idx_dim)
            o_a_s = dsa_attn_sample(q_a, score_s, dsa_rows, bias_s, cache_dsa, j, page_table, m_p, t_s, attn_scale, hd)
            o_b_p = mla_prompt(q_cat, mla_rows, w_uv_heads, n_p, t_p, mla_scale)
            o_b_s = mla_sample(q_cat, mla_rows, cache_mla, j, page_table, w_uv_heads, m_p, t_s, mla_scale)
            heads = jnp.concatenate([jnp.concatenate([o_a_p, o_a_s.astype(MXU_DTYPE)], axis=0),
                                     jnp.concatenate([o_b_p, o_b_s.astype(MXU_DTYPE)], axis=0)], axis=1)
            mix = matmul(heads, w_o_ab[j], out_dtype=F32)
            new_dsa.append(dsa_rows)
            new_mla.append(mla_rows)
        else:
            w_in = w_in_c[j]
            q = matmul(xb, w_in, out_dtype=F32, col_start=0, n_cols=h_c * hd)
            kv_c = matmul(xb, w_in, out_dtype=F32, col_start=h_c * hd, n_cols=2 * hd)
            logf = fox_log_forget(xb, w_in[:, h_c * hd + 2 * hd:], b_forget[j])
            cum_p = cumsum_prompt(logf[:m_p], n_p, t_p)
            cum_t_p = jnp.swapaxes(cum_p.reshape(n_p, t_p, h_c), 1, 2)
            o_p = fox_prompt(q, kv_c, cum_p, cum_t_p, n_p, t_p, attn_scale)
            cum_past, cum_new = cumsum_sample(logf, cache_fox_logf, j, page_table, m_p, t_s)
            cum_t_new = jnp.pad(jnp.swapaxes(cum_new, 1, 2), ((0, 0), (0, 0), (0, LANES - t_s)))
            o_s = fox_sample(q, kv_c, cum_new, jnp.swapaxes(cum_past, 1, 2), cum_t_new,
                             cache_fox_kv, j, page_table, m_p, t_s, attn_scale)
            heads = jnp.concatenate([o_p, o_s.astype(MXU_DTYPE)], axis=0)
            mix = matmul(heads, w_o_c[j], out_dtype=F32)
            new_fkv.append(kv_c)
            new_flf.append(logf)
        h, hb = deepnorm(x, mix, ln1_g[i], ln1_b[i], alpha)
        if i % 2 == 0:
            ff = matmul(swiglu_up(hb, w_ffn_up[j]), w_ffn_down[j], out_dtype=F32, tk=w_ffn_down.shape[1] // 2)
        else:
            combine = router_combine(hb, w_router[j])
            ff = moe_down(moe_up(hb, w_moe_up[j]), w_moe_down[j], jnp.transpose(combine)[:, :, None])
        h, hb = deepnorm(h, ff, ln2_g[i], ln2_b[i], alpha)
        x, xb = ple_mix(h, hb, w_ple_gate[i], b_ple_gate[i], ple[i], w_ple_proj[i])

    def split(rows_list):
        a = jnp.stack(rows_list)
        return a[:, :m_p].reshape(len(rows_list), n_p, t_p, -1), a[:, m_p:].reshape(len(rows_list), n_s, t_s, -1)

    dsa_p, dsa_s = split(new_dsa)
    mla_p, mla_s = split(new_mla)
    fkv_p, fkv_s = split(new_fkv)
    flf_p, flf_s = split(new_flf)
    return (x[:m_p].reshape(n_p, t_p, d), x[m_p:].reshape(n_s, t_s, d),
            dsa_p, mla_p, fkv_p, flf_p, dsa_s, mla_s, fkv_s, flf_s)
```

```python
import functools
import math

import numpy as np
import jax
import jax.numpy as jnp
from jax import lax
from jax.experimental import pallas as pl
from jax.experimental.pallas import tpu as pltpu

DSA_TOPK = 256
N_BUCKETS = 32
MAX_DISTANCE = 128
ROPE_THETA = 10000.0
LN_EPS = 1e-5
RMS_EPS = 1e-6
MOE_TOP_K = 2

MXU_DTYPE = jnp.bfloat16
MASK_VALUE = -1e30
V7X_VMEM_BYTES = 64 << 20
VMEM_CAP_BYTES = V7X_VMEM_BYTES - (6 << 20)
LANES = 128
F32 = jnp.float32


def _tile(n, target, align=8):
    if n <= target:
        return n
    for d in range(target, 0, -1):
        if n % d == 0 and d % align == 0:
            return d
    return n


def _params(semantics, vmem_bytes):
    limit = int(min(max(vmem_bytes, 16 << 20), VMEM_CAP_BYTES))
    return pltpu.CompilerParams(dimension_semantics=semantics, vmem_limit_bytes=limit)


def _nt_dot(a, b):
    return lax.dot_general(a, b, (((1,), (1,)), ((), ())), preferred_element_type=F32)


def _dot(a, b):
    return jnp.dot(a, b, preferred_element_type=F32)


def _mm_kernel(x_ref, w_ref, o_ref, *scratch, nk):
    part = _dot(x_ref[...].astype(MXU_DTYPE), w_ref[...].astype(MXU_DTYPE))
    if nk == 1:
        o_ref[...] = part.astype(o_ref.dtype)
        return
    acc_ref, = scratch
    k = pl.program_id(2)

    @pl.when(k == 0)
    def _():
        acc_ref[...] = part

    @pl.when(k > 0)
    def _():
        acc_ref[...] += part

    @pl.when(k == nk - 1)
    def _():
        o_ref[...] = acc_ref[...].astype(o_ref.dtype)


def matmul(x, w, *, out_dtype, col_start=0, n_cols=None, tm=1024, tn=256, tk=4096):
    m, kdim = x.shape
    n_cols = w.shape[1] - col_start if n_cols is None else n_cols
    tm = _tile(m, tm)
    tk = _tile(kdim, tk, LANES)
    if n_cols % LANES == 0 and col_start % LANES == 0:
        tn = _tile(n_cols, tn, LANES)
        while col_start % tn:
            tn -= LANES
    else:
        assert col_start == 0 and n_cols == w.shape[1]
        tn = n_cols
    off = col_start // tn
    xb, ob = x.dtype.itemsize, jnp.dtype(out_dtype).itemsize

    def estimate(tk):
        return 2 * (tm * tk * xb + tk * tn * 4 + tm * tn * ob) + tm * tn * 4 + tk * tn * 2 + tm * tk * 2 + (4 << 20)

    while estimate(tk) > VMEM_CAP_BYTES * 3 // 4 and tk % (2 * LANES) == 0:
        tk //= 2
    nk = kdim // tk
    vmem = estimate(tk)
    return pl.pallas_call(
        functools.partial(_mm_kernel, nk=nk),
        grid=(m // tm, n_cols // tn, nk),
        in_specs=[pl.BlockSpec((tm, tk), lambda i, j, k: (i, k)),
                  pl.BlockSpec((tk, tn), lambda i, j, k: (k, j + off))],
        out_specs=pl.BlockSpec((tm, tn), lambda i, j, k: (i, j)),
        out_shape=jax.ShapeDtypeStruct((m, n_cols), out_dtype),
        scratch_shapes=[pltpu.VMEM((tm, tn), F32)] if nk > 1 else [],
        compiler_params=_params(("parallel", "parallel", "arbitrary"), vmem),
        name="matmul",
    )(x, w)


def _swiglu_up_kernel(x_ref, wg_ref, wu_ref, o_ref):
    x = x_ref[...]
    gate = _dot(x, wg_ref[...].astype(MXU_DTYPE))
    up = _dot(x, wu_ref[...].astype(MXU_DTYPE))
    o_ref[...] = (gate * jax.nn.sigmoid(gate) * up).astype(o_ref.dtype)


def swiglu_up(xb, w_up, *, tm=1024, tn=256):
    m, d = xb.shape
    f = w_up.shape[1] // 2
    tm, tn = _tile(m, tm), _tile(f, tn, LANES)
    nf = f // tn
    vmem = 2 * (tm * d * 2 + 2 * d * tn * 4 + tm * tn * 2) + 2 * d * tn * 2 + 3 * tm * tn * 4 + (4 << 20)
    return pl.pallas_call(
        _swiglu_up_kernel,
        grid=(m // tm, nf),
        in_specs=[pl.BlockSpec((tm, d), lambda i, j: (i, 0)),
                  pl.BlockSpec((d, tn), lambda i, j: (0, j)),
                  pl.BlockSpec((d, tn), lambda i, j: (0, j + nf))],
        out_specs=pl.BlockSpec((tm, tn), lambda i, j: (i, j)),
        out_shape=jax.ShapeDtypeStruct((m, f), MXU_DTYPE),
        compiler_params=_params(("parallel", "parallel"), vmem),
        name="swiglu_up",
    )(xb, w_up, w_up)


def moe_up(xb, w_up, *, tm=1024, tn=256):
    m, d = xb.shape
    n_exp, _, f2 = w_up.shape
    f = f2 // 2
    tm, tn = _tile(m, tm), _tile(f, tn, LANES)
    nf = f // tn
    vmem = 2 * (tm * d * 2 + 2 * d * tn * 4 + tm * tn * 2) + 2 * d * tn * 2 + 3 * tm * tn * 4 + (4 << 20)
    return pl.pallas_call(
        _swiglu_up_kernel,
        grid=(m // tm, n_exp, nf),
        in_specs=[pl.BlockSpec((tm, d), lambda i, e, j: (i, 0)),
                  pl.BlockSpec((None, d, tn), lambda i, e, j: (e, 0, j)),
                  pl.BlockSpec((None, d, tn), lambda i, e, j: (e, 0, j + nf))],
        out_specs=pl.BlockSpec((tm, tn), lambda i, e, j: (i, e * nf + j)),
        out_shape=jax.ShapeDtypeStruct((m, n_exp * f), MXU_DTYPE),
        compiler_params=_params(("parallel", "parallel", "parallel"), vmem),
        name="moe_up",
    )(xb, w_up, w_up)


def _moe_down_kernel(a_ref, w_ref, c_ref, o_ref, acc_ref, *, n_exp):
    e = pl.program_id(2)
    part = c_ref[...] * _dot(a_ref[...], w_ref[...].astype(MXU_DTYPE))

    @pl.when(e == 0)
    def _():
        acc_ref[...] = part

    @pl.when(e > 0)
    def _():
        acc_ref[...] += part

    @pl.when(e == n_exp - 1)
    def _():
        o_ref[...] = acc_ref[...]


def moe_down(act, w_down, combine_t, *, tm=1024, tn=512):
    m = act.shape[0]
    n_exp, f, d = w_down.shape
    tm, tn = _tile(m, tm), _tile(d, tn, LANES)
    vmem = 2 * (tm * f * 2 + f * tn * 4 + tm * LANES * 4 + tm * tn * 4) + 3 * tm * tn * 4 + f * tn * 2 + (4 << 20)
    return pl.pallas_call(
        functools.partial(_moe_down_kernel, n_exp=n_exp),
        grid=(m // tm, d // tn, n_exp),
        in_specs=[pl.BlockSpec((tm, f), lambda i, j, e: (i, e)),
                  pl.BlockSpec((None, f, tn), lambda i, j, e: (e, 0, j)),
                  pl.BlockSpec((None, tm, 1), lambda i, j, e: (e, i, 0))],
        out_specs=pl.BlockSpec((tm, tn), lambda i, j, e: (i, j)),
        out_shape=jax.ShapeDtypeStruct((m, d), F32),
        scratch_shapes=[pltpu.VMEM((tm, tn), F32)],
        compiler_params=_params(("parallel", "parallel", "arbitrary"), vmem),
        name="moe_down",
    )(act, w_down, combine_t)


def _router_kernel(x_ref, w_ref, o_ref):
    logits = _dot(x_ref[...], w_ref[...].astype(MXU_DTYPE))
    n_exp = logits.shape[1]
    lane = lax.broadcasted_iota(jnp.int32, logits.shape, 1)
    m1 = jnp.max(logits, axis=1, keepdims=True)
    i1 = jnp.min(jnp.where(logits == m1, lane, n_exp), axis=1, keepdims=True)
    rest = jnp.where(lane == i1, -jnp.inf, logits)
    m2 = jnp.max(rest, axis=1, keepdims=True)
    i2 = jnp.min(jnp.where(rest == m2, lane, n_exp), axis=1, keepdims=True)
    e2 = jnp.exp(m2 - m1)
    denom = 1.0 + e2
    o_ref[...] = jnp.where(lane == i1, 1.0 / denom, 0.0) + jnp.where(lane == i2, e2 / denom, 0.0)


def router_combine(xb, w_router, *, tm=1024):
    m, d = xb.shape
    n_exp = w_router.shape[1]
    tm = _tile(m, tm)
    vmem = 2 * (tm * d * 2 + d * LANES * 4 + tm * LANES * 4) + (8 << 20)
    return pl.pallas_call(
        _router_kernel,
        grid=(m // tm,),
        in_specs=[pl.BlockSpec((tm, d), lambda i: (i, 0)),
                  pl.BlockSpec((d, n_exp), lambda i: (0, 0))],
        out_specs=pl.BlockSpec((tm, n_exp), lambda i: (i, 0)),
        out_shape=jax.ShapeDtypeStruct((m, n_exp), F32),
        compiler_params=_params(("parallel",), vmem),
    )(xb, w_router)


def _deepnorm_kernel(x_ref, y_ref, g_ref, b_ref, o_ref, ob_ref, *, alpha):
    z = alpha * x_ref[...] + y_ref[...]
    mu = jnp.mean(z, axis=-1, keepdims=True)
    zc = z - mu
    var = jnp.mean(zc * zc, axis=-1, keepdims=True)
    out = zc * lax.rsqrt(var + LN_EPS) * g_ref[...] + b_ref[...]
    o_ref[...] = out
    ob_ref[...] = out.astype(ob_ref.dtype)


def deepnorm(x, y, g, b, alpha, *, tm=256):
    m, d = x.shape
    tm = _tile(m, tm)
    row = pl.BlockSpec((tm, d), lambda i: (i, 0))
    vec = pl.BlockSpec((1, d), lambda i: (0, 0))
    return pl.pallas_call(
        functools.partial(_deepnorm_kernel, alpha=alpha),
        grid=(m // tm,),
        in_specs=[row, row, vec, vec],
        out_specs=[row, row],
        out_shape=[jax.ShapeDtypeStruct((m, d), F32), jax.ShapeDtypeStruct((m, d), MXU_DTYPE)],
        compiler_params=_params(("parallel",), 12 * tm * d * 4 + (4 << 20)),
    )(x, y, g.reshape(1, d), b.reshape(1, d))


def _ple_kernel(hb_ref, wg_ref, bg_ref, p_ref, wp_ref, h_ref, o_ref, ob_ref):
    gate = jax.nn.sigmoid(_dot(hb_ref[...], wg_ref[...].astype(MXU_DTYPE)) + bg_ref[...])
    emb = _dot(p_ref[...].astype(MXU_DTYPE), wp_ref[...].astype(MXU_DTYPE))
    out = h_ref[...] + gate * emb
    o_ref[...] = out
    ob_ref[...] = out.astype(ob_ref.dtype)


def ple_mix(h, hb, w_gate, b_gate, p, w_proj, *, tm=1024, tn=256):
    m, d = h.shape
    pd = p.shape[1]
    tm, tn = _tile(m, tm), _tile(d, tn, LANES)
    vmem = 2 * (tm * d * 2 + d * tn * 4 + tm * pd * 4 + pd * tn * 4 + 3 * tm * tn * 4) + d * tn * 2 + 4 * tm * tn * 4 + (4 << 20)
    tile = pl.BlockSpec((tm, tn), lambda i, j: (i, j))
    return pl.pallas_call(
        _ple_kernel,
        grid=(m // tm, d // tn),
        in_specs=[pl.BlockSpec((tm, d), lambda i, j: (i, 0)),
                  pl.BlockSpec((d, tn), lambda i, j: (0, j)),
                  pl.BlockSpec((1, tn), lambda i, j: (0, j)),
                  pl.BlockSpec((tm, pd), lambda i, j: (i, 0)),
                  pl.BlockSpec((pd, tn), lambda i, j: (0, j)),
                  tile],
        out_specs=[tile, tile],
        out_shape=[jax.ShapeDtypeStruct((m, d), F32), jax.ShapeDtypeStruct((m, d), MXU_DTYPE)],
        compiler_params=_params(("parallel", "parallel"), vmem),
    )(hb, w_gate, b_gate.reshape(1, d), p, w_proj, h)


def _rms(x, g):
    return x * lax.rsqrt(jnp.mean(x * x, axis=-1, keepdims=True) + RMS_EPS) * g


def _ab_post_kernel(cq_ref, ckv_ref, k1_ref, k2_ref, cos_ref, sin_ref, gq_ref, gkv_ref,
                    cqn_ref, ckvn_ref, r1_ref, r2_ref):
    cqn_ref[...] = _rms(cq_ref[...], gq_ref[...]).astype(cqn_ref.dtype)
    ckvn_ref[...] = _rms(ckv_ref[...], gkv_ref[...])
    x1, x2, cos, sin = k1_ref[...], k2_ref[...], cos_ref[...], sin_ref[...]
    r1_ref[...] = x1 * cos - x2 * sin
    r2_ref[...] = x1 * sin + x2 * cos


def ab_post(c_q, c_kv, k_r, cos, sin, q_norm, kv_norm, *, tm=512):
    m, ql = c_q.shape
    kl, half = c_kv.shape[1], cos.shape[1]
    tm = _tile(m, tm)

    def row(n):
        return pl.BlockSpec((tm, n), lambda i: (i, 0))

    def vec(n):
        return pl.BlockSpec((1, n), lambda i: (0, 0))

    cqn, ckvn, r1, r2 = pl.pallas_call(
        _ab_post_kernel,
        grid=(m // tm,),
        in_specs=[row(ql), row(kl), row(half), row(half), row(half), row(half), vec(ql), vec(kl)],
        out_specs=[row(ql), row(kl), row(half), row(half)],
        out_shape=[jax.ShapeDtypeStruct((m, ql), MXU_DTYPE), jax.ShapeDtypeStruct((m, kl), F32),
                   jax.ShapeDtypeStruct((m, half), F32), jax.ShapeDtypeStruct((m, half), F32)],
        compiler_params=_params(("parallel",), 32 << 20),
    )(c_q, c_kv, k_r[:, :half], k_r[:, half:], cos, sin, q_norm.reshape(1, ql), kv_norm.reshape(1, kl))
    return cqn, jnp.concatenate([ckvn, r1, r2], axis=-1)


def _mla_q_kernel(c_ref, wn_ref, w1_ref, w2_ref, wk_ref, cos_ref, sin_ref, o_ref, *, kl, half):
    c = c_ref[...]
    q_nope = _dot(c, wn_ref[...].astype(MXU_DTYPE))
    x1 = _dot(c, w1_ref[...].astype(MXU_DTYPE))
    x2 = _dot(c, w2_ref[...].astype(MXU_DTYPE))
    cos, sin = cos_ref[...], sin_ref[...]
    o_ref[:, 0:kl] = _nt_dot(q_nope.astype(MXU_DTYPE), wk_ref[...].astype(MXU_DTYPE))
    o_ref[:, kl:kl + half] = x1 * cos - x2 * sin
    o_ref[:, kl + half:kl + 2 * half] = x1 * sin + x2 * cos


def mla_queries(cqn, w_uq, w_uk, cos, sin, *, tm=512):
    m, ql = cqn.shape
    kl, n_heads, nope = w_uk.shape
    half = cos.shape[1]
    tm = _tile(m, tm)
    w_heads = jnp.transpose(w_uq, (1, 0, 2))
    wk_heads = jnp.transpose(w_uk, (1, 0, 2))
    return pl.pallas_call(
        functools.partial(_mla_q_kernel, kl=kl, half=half),
        grid=(m // tm, n_heads),
        in_specs=[pl.BlockSpec((tm, ql), lambda i, h: (i, 0)),
                  pl.BlockSpec((None, ql, nope), lambda i, h: (h, 0, 0)),
                  pl.BlockSpec((None, ql, half), lambda i, h: (h, 0, 0)),
                  pl.BlockSpec((None, ql, half), lambda i, h: (h, 0, 0)),
                  pl.BlockSpec((None, kl, nope), lambda i, h: (h, 0, 0)),
                  pl.BlockSpec((tm, half), lambda i, h: (i, 0)),
                  pl.BlockSpec((tm, half), lambda i, h: (i, 0))],
        out_specs=pl.BlockSpec((None, tm, kl + 2 * half), lambda i, h: (h, i, 0)),
        out_shape=jax.ShapeDtypeStruct((n_heads, m, kl + 2 * half), F32),
        compiler_params=_params(("parallel", "parallel"), 32 << 20),
    )(cqn, w_heads[:, :, :nope], w_heads[:, :, nope:nope + half], w_heads[:, :, nope + half:], wk_heads, cos, sin)


def _fox_gate_kernel(x_ref, w_ref, b_ref, o_ref):
    z = _dot(x_ref[...], w_ref[...].astype(MXU_DTYPE)) + b_ref[...]
    o_ref[...] = jnp.minimum(z, 0.0) - jnp.log1p(jnp.exp(-jnp.abs(z)))


def fox_log_forget(xb, w_f, b_f, *, tm=1024):
    m, d = xb.shape
    h = w_f.shape[1]
    tm = _tile(m, tm)
    return pl.pallas_call(
        _fox_gate_kernel,
        grid=(m // tm,),
        in_specs=[pl.BlockSpec((tm, d), lambda i: (i, 0)),
                  pl.BlockSpec((d, h), lambda i: (0, 0)),
                  pl.BlockSpec((1, h), lambda i: (0, 0))],
        out_specs=pl.BlockSpec((tm, h), lambda i: (i, 0)),
        out_shape=jax.ShapeDtypeStruct((m, h), F32),
        compiler_params=_params(("parallel",), 2 * (tm * d * 2 + d * LANES * 4 + tm * LANES * 4) + (8 << 20)),
    )(xb, w_f, b_f.reshape(1, h))


def _softmax_init(m_ref, l_ref, acc_ref):
    m_ref[...] = jnp.full(m_ref.shape, MASK_VALUE, F32)
    l_ref[...] = jnp.zeros(l_ref.shape, F32)
    acc_ref[...] = jnp.zeros(acc_ref.shape, F32)


def _softmax_step(s, v, m_ref, l_ref, acc_ref):
    m_old = m_ref[...]
    m_new = jnp.maximum(m_old, jnp.max(s, axis=1, keepdims=True))
    alpha = jnp.exp(m_old - m_new)
    p = jnp.exp(s - m_new)
    l_ref[...] = alpha * l_ref[...] + jnp.sum(p, axis=1, keepdims=True)
    acc_ref[...] = alpha * acc_ref[...] + _dot(p.astype(MXU_DTYPE), v)
    m_ref[...] = m_new


def _softmax_step_cols(logits_of_head, v_t, n_heads, tq, m_ref, l_ref, acc_ref, p_ref):
    for h in range(n_heads):
        c = slice(h * tq, (h + 1) * tq)
        s = logits_of_head(h)
        m_old = m_ref[:, c]
        m_new = jnp.maximum(m_old, jnp.max(s, axis=0, keepdims=True))
        alpha = jnp.exp(m_old - m_new)
        p = jnp.exp(s - m_new)
        l_ref[:, c] = alpha * l_ref[:, c] + jnp.sum(p, axis=0, keepdims=True)
        m_ref[:, c] = m_new
        acc_ref[:, c] = alpha * acc_ref[:, c]
        p_ref[:, c] = p.astype(p_ref.dtype)
    acc_ref[...] += _dot(v_t, p_ref[...])


def _cols_to_heads(o_ref, o_t, n_heads, tq, dv):
    for h in range(n_heads):
        o_ref[:, h * dv:(h + 1) * dv] = o_t[:, h * tq:(h + 1) * tq].T.astype(o_ref.dtype)


def _heads_to_rows(q, n_heads, hd):
    return jnp.concatenate([q[:, h * hd:(h + 1) * hd] for h in range(n_heads)], axis=0)


def _rows_to_heads(o_ref, o, n_heads, t, hd):
    for h in range(n_heads):
        o_ref[:, h * hd:(h + 1) * hd] = o[h * t:(h + 1) * t, :].astype(o_ref.dtype)


def _prefix_sum_rows(x, carry):
    n = x.shape[0]
    tri = (lax.broadcasted_iota(jnp.int32, (n, n), 1) <= lax.broadcasted_iota(jnp.int32, (n, n), 0)).astype(MXU_DTYPE)
    hi = x.astype(MXU_DTYPE)
    r1 = x - hi.astype(F32)
    mid = r1.astype(MXU_DTYPE)
    lo = (r1 - mid.astype(F32)).astype(MXU_DTYPE)
    return _dot(tri, hi) + _dot(tri, mid) + _dot(tri, lo) + carry


def _order_key(score):
    bits = lax.bitcast_convert_type(score, jnp.int32)
    return jnp.where(bits < 0, bits ^ jnp.int32(0x7FFFFFFF), bits)


def _kth_largest_key(key, k, axis=1):
    shape = tuple(1 if a == axis else n for a, n in enumerate(key.shape))

    step = 8 if axis == 0 else LANES

    def count_ge(t):
        hit = jnp.where(key >= t, 1.0, 0.0)
        parts = [lax.slice_in_dim(hit, s, s + step, axis=axis) for s in range(0, key.shape[axis], step)]
        while len(parts) > 1:
            parts = [a + b for a, b in zip(parts[0::2], parts[1::2])] + parts[len(parts) - len(parts) % 2:]
        return jnp.sum(parts[0], axis=axis, keepdims=True)

    int_min = jnp.int32(-2 ** 31)
    thr = jnp.where(count_ge(jnp.zeros(shape, jnp.int32)) >= k, jnp.int32(0), int_min)

    def body(it, thr):
        cand = thr | lax.shift_left(jnp.int32(1), jnp.int32(30) - it)
        return jnp.where(count_ge(cand) >= k, cand, thr)

    return lax.fori_loop(0, 31, body, thr)


def bias_tables(rel_bias, tq):
    max_exact = N_BUCKETS // 2
    d = np.arange(0, 2 * LANES + tq)
    ratio = np.log(np.maximum(d, 1).astype(np.float32) / max_exact) / math.log(MAX_DISTANCE / max_exact)
    large = np.minimum(max_exact + (ratio * (N_BUCKETS - max_exact)).astype(np.int32), N_BUCKETS - 1)
    bucket = np.where(d < max_exact, d, large)
    i = np.arange(tq)[:, None]
    j = np.arange(LANES)[None, :]
    dist = np.stack([np.maximum(i - j, 0), i - j + LANES, np.full((tq, LANES), 2 * LANES)])
    return jnp.transpose(rel_bias[bucket[dist]], (0, 3, 1, 2)).astype(F32)


def _mla_out_kernel(o_ref, w_ref, out_ref):
    out_ref[...] = _dot(o_ref[...].astype(MXU_DTYPE), w_ref[...].astype(MXU_DTYPE)).astype(out_ref.dtype)


def mla_out(o_lat, w_uv_heads, *, tm=1024):
    m = o_lat.shape[0]
    n_heads, dv, hd = w_uv_heads.shape
    tm = _tile(m, tm)
    return pl.pallas_call(
        _mla_out_kernel,
        grid=(m // tm, n_heads),
        in_specs=[pl.BlockSpec((tm, dv), lambda i, h: (i, h)),
                  pl.BlockSpec((None, dv, hd), lambda i, h: (h, 0, 0))],
        out_specs=pl.BlockSpec((tm, hd), lambda i, h: (i, h)),
        out_shape=jax.ShapeDtypeStruct((m, n_heads * hd), MXU_DTYPE),
        compiler_params=_params(("parallel", "parallel"), 24 << 20),
        name="mla_out",
    )(o_lat, w_uv_heads)


def _causal_chunks(i, tq, tk, step):
    n_full = (i * tq + 1) // tk

    def body(kb, carry):
        step(kb, False)
        return carry

    lax.fori_loop(0, n_full, body, 0)
    step(n_full, True)


def _key_visible(kb, tk, i, tq):
    k_pos = kb * tk + lax.broadcasted_iota(jnp.int32, (tk, 1), 0)
    q_pos = i * tq + lax.broadcasted_iota(jnp.int32, (1, tq), 1)
    return k_pos <= q_pos


def _mla_prompt_kernel(q_ref, k_ref, vt_ref, o_ref, m_ref, l_ref, acc_ref, p_ref, *, tq, tk, scale):
    i = pl.program_id(1)
    n_heads, _, dk = q_ref.shape
    dv = vt_ref.shape[0]
    q = q_ref[...].reshape(n_heads * tq, dk).astype(MXU_DTYPE)
    _softmax_init(m_ref, l_ref, acc_ref)

    def step(kb, masked):
        start = pl.multiple_of(kb * tk, tk)
        k = k_ref[pl.ds(start, tk), :].astype(MXU_DTYPE)
        v_t = vt_ref[:, pl.ds(start, tk)].astype(MXU_DTYPE)
        raw = _nt_dot(k, q)
        visible = _key_visible(kb, tk, i, tq)

        def logits(h):
            s = raw[:, h * tq:(h + 1) * tq] * scale
            return jnp.where(visible, s, MASK_VALUE) if masked else s

        _softmax_step_cols(logits, v_t, n_heads, tq, m_ref, l_ref, acc_ref, p_ref)

    _causal_chunks(i, tq, tk, step)
    _cols_to_heads(o_ref, acc_ref[...] / l_ref[...], n_heads, tq, dv)


def mla_prompt(q_cat, rows, v_t, n_seq, t, scale, *, tq=128, tk=256):
    n_heads, _, dk = q_cat.shape
    dv = v_t.shape[0]
    tq, tk = _tile(t, tq), _tile(t, tk)
    assert tk % tq == 0
    nq = t // tq
    r = n_heads * tq
    vmem = (2 * (r * dk * 4 + t * dk * 4 + t * dv * 4 + tq * n_heads * dv * 2)
            + r * dk * 2 + 3 * r * dv * 4 + 4 * r * tk * 4 + (6 << 20))
    return pl.pallas_call(
        functools.partial(_mla_prompt_kernel, tq=tq, tk=tk, scale=scale),
        grid=(n_seq, nq),
        in_specs=[pl.BlockSpec((n_heads, tq, dk), lambda b, i: (0, b * nq + i, 0)),
                  pl.BlockSpec((t, dk), lambda b, i: (b, 0)),
                  pl.BlockSpec((dv, t), lambda b, i: (0, b))],
        out_specs=pl.BlockSpec((tq, n_heads * dv), lambda b, i: (b * nq + i, 0)),
        out_shape=jax.ShapeDtypeStruct((n_seq * t, n_heads * dv), MXU_DTYPE),
        scratch_shapes=[pltpu.VMEM((1, r), F32), pltpu.VMEM((1, r), F32), pltpu.VMEM((dv, r), F32),
                        pltpu.VMEM((tk, r), MXU_DTYPE)],
        compiler_params=_params(("parallel", "arbitrary"), vmem),
        name="mla_prompt",
    )(q_cat, rows, v_t)


def _fox_prompt_kernel(q_ref, kv_ref, vt_ref, cum_ref, cumt_ref, o_ref, m_ref, l_ref, acc_ref, p_ref,
                       *, tq, tk, scale, hd):
    i = pl.program_id(1)
    n_heads = q_ref.shape[1] // hd
    q = _heads_to_rows(q_ref[...].astype(MXU_DTYPE), n_heads, hd)
    cum_q = cumt_ref[:, pl.ds(pl.multiple_of(i * tq, tq), tq)]
    _softmax_init(m_ref, l_ref, acc_ref)

    def step(kb, masked):
        start = pl.multiple_of(kb * tk, tk)
        k = kv_ref[pl.ds(start, tk), 0:hd].astype(MXU_DTYPE)
        v_t = vt_ref[:, pl.ds(start, tk)].astype(MXU_DTYPE)
        cum_k = cum_ref[pl.ds(start, tk), :]
        raw = _nt_dot(k, q)
        visible = _key_visible(kb, tk, i, tq)

        def logits(h):
            s = raw[:, h * tq:(h + 1) * tq] * scale + cum_q[h:h + 1, :] - cum_k[:, h:h + 1]
            return jnp.where(visible, s, MASK_VALUE) if masked else s

        _softmax_step_cols(logits, v_t, n_heads, tq, m_ref, l_ref, acc_ref, p_ref)

    _causal_chunks(i, tq, tk, step)
    _cols_to_heads(o_ref, acc_ref[...] / l_ref[...], n_heads, tq, hd)


def fox_prompt(q, kv_rows, v_t, cum, cum_t, n_seq, t, scale, *, tq=128, tk=256):
    width = q.shape[1]
    hd = kv_rows.shape[1] // 2
    n_heads = width // hd
    tq, tk = _tile(t, tq), _tile(t, tk)
    assert tk % tq == 0
    nq = t // tq
    r = n_heads * tq
    vmem = (2 * (2 * tq * width * 2 + t * 2 * hd * 4 + tq * LANES * 4 + n_heads * t * 4)
            + r * hd * 2 + 2 * r * LANES * 4 + 2 * r * hd * 4 + 5 * r * tk * 4 + (6 << 20))
    return pl.pallas_call(
        functools.partial(_fox_prompt_kernel, tq=tq, tk=tk, scale=scale, hd=hd),
        grid=(n_seq, nq),
        in_specs=[pl.BlockSpec((tq, width), lambda b, i: (b * nq + i, 0)),
                  pl.BlockSpec((t, 2 * hd), lambda b, i: (b, 0)),
                  pl.BlockSpec((hd, t), lambda b, i: (0, b)),
                  pl.BlockSpec((t, n_heads), lambda b, i: (b, 0)),
                  pl.BlockSpec((None, n_heads, t), lambda b, i: (b, 0, 0))],
        out_specs=pl.BlockSpec((tq, width), lambda b, i: (b * nq + i, 0)),
        out_shape=jax.ShapeDtypeStruct((n_seq * t, width), MXU_DTYPE),
        scratch_shapes=[pltpu.VMEM((1, r), F32), pltpu.VMEM((1, r), F32), pltpu.VMEM((hd, r), F32),
                        pltpu.VMEM((tk, r), MXU_DTYPE)],
        compiler_params=_params(("parallel", "arbitrary"), vmem),
        name="fox_prompt",
    )(q, kv_rows, v_t, cum, cum_t)


def _cumsum_prompt_kernel(x_ref, o_ref, *, blk):
    t, c = x_ref.shape
    carry = jnp.zeros((1, c), F32)
    for s in range(0, t, blk):
        out = _prefix_sum_rows(x_ref[s:s + blk, :], carry)
        o_ref[s:s + blk, :] = out
        carry = out[blk - 1:blk, :]


def cumsum_prompt(logf, n_seq, t):
    c = logf.shape[1]
    blk = _tile(t, LANES)
    return pl.pallas_call(
        functools.partial(_cumsum_prompt_kernel, blk=blk),
        grid=(n_seq,),
        in_specs=[pl.BlockSpec((t, c), lambda b: (b, 0))],
        out_specs=pl.BlockSpec((t, c), lambda b: (b, 0)),
        out_shape=jax.ShapeDtypeStruct((n_seq * t, c), F32),
        compiler_params=_params(("parallel",), 16 << 20),
    )(logf)


def _dsa_prompt_kernel(qa_ref, qi_ref, wit_ref, rows_ref, vt_ref, bias_ref, o_ref,
                       score_ref, m_ref, l_ref, acc_ref, p_ref, *, tq, tk, topk, scale, w_scale, hd, idx_dim):
    i = pl.program_id(1)
    n_heads = qa_ref.shape[1] // hd
    n_idx = qi_ref.shape[1] // idx_dim
    n_chunks = (i * tq + tq + tk - 1) // tk

    score_ref[...] = jnp.full(score_ref.shape, -jnp.inf, F32)
    qi = _heads_to_rows(qi_ref[...].astype(MXU_DTYPE), n_idx, idx_dim)
    wi = wit_ref[...] * w_scale

    def score_step(kc, _):
        start = pl.multiple_of(kc * tk, tk)
        ki = rows_ref[pl.ds(start, tk), 2 * hd:2 * hd + idx_dim].astype(MXU_DTYPE)
        act = jnp.maximum(_nt_dot(ki, qi), 0.0)
        acc = jnp.zeros((tk, tq), F32)
        for h in range(n_idx):
            acc = acc + wi[h:h + 1, :] * act[:, h * tq:(h + 1) * tq]
        score_ref[pl.ds(start, tk), :] = jnp.where(_key_visible(kc, tk, i, tq), acc, -jnp.inf)
        return 0

    lax.fori_loop(0, n_chunks, score_step, 0)

    score = score_ref[...]
    key = _order_key(score)
    thr = _kth_largest_key(key, topk, axis=0)
    score_ref[...] = jnp.where((key >= thr) & (score > -jnp.inf), 0.0, MASK_VALUE)

    q = _heads_to_rows(qa_ref[...].astype(MXU_DTYPE), n_heads, hd)
    _softmax_init(m_ref, l_ref, acc_ref)
    per_step = tk // LANES

    def attn_step(kb, _):
        start = pl.multiple_of(kb * tk, tk)
        k = rows_ref[pl.ds(start, tk), 0:hd].astype(MXU_DTYPE)
        v_t = vt_ref[:, pl.ds(start, tk)].astype(MXU_DTYPE)
        keep = score_ref[pl.ds(start, tk), :]
        raw = _nt_dot(k, q)
        behind = [jnp.clip(i - (kb * per_step + b), 0, 2) for b in range(per_step)]

        def logits(h):
            bias = jnp.concatenate([bias_ref[off, h] for off in behind], axis=0)
            return raw[:, h * tq:(h + 1) * tq] * scale + bias + keep

        _softmax_step_cols(logits, v_t, n_heads, tq, m_ref, l_ref, acc_ref, p_ref)
        return 0

    lax.fori_loop(0, n_chunks, attn_step, 0)
    _cols_to_heads(o_ref, acc_ref[...] / l_ref[...], n_heads, tq, hd)


def dsa_prompt(q_a, q_i, w_i_t, rows, v_t, bias_t, n_seq, t, scale, w_scale, hd, idx_dim):
    tq = LANES
    tk = 2 * LANES
    assert t % tk == 0
    nq = t // tq
    n_heads = q_a.shape[1] // hd
    r = n_heads * tq
    row_w = rows.shape[1]
    vmem = (2 * (2 * tq * q_a.shape[1] * 4 + tq * q_i.shape[1] * 4 + t * (row_w + hd) * 4 + bias_t.size * 4)
            + 4 * tq * t * 4 + 3 * r * hd * 4 + 3 * tk * q_i.shape[1] // idx_dim * tq * 4 + 3 * tk * r * 4 + (6 << 20))
    return pl.pallas_call(
        functools.partial(_dsa_prompt_kernel, tq=tq, tk=tk, topk=min(DSA_TOPK, t // 4), scale=scale,
                          w_scale=w_scale, hd=hd, idx_dim=idx_dim),
        grid=(n_seq, nq),
        in_specs=[pl.BlockSpec((tq, q_a.shape[1]), lambda b, i: (b * nq + i, 0)),
                  pl.BlockSpec((tq, q_i.shape[1]), lambda b, i: (b * nq + i, 0)),
                  pl.BlockSpec((w_i_t.shape[0], tq), lambda b, i: (0, b * nq + i)),
                  pl.BlockSpec((t, row_w), lambda b, i: (b, 0)),
                  pl.BlockSpec((hd, t), lambda b, i: (0, b)),
                  pl.BlockSpec(bias_t.shape, lambda b, i: (0, 0, 0, 0))],
        out_specs=pl.BlockSpec((tq, q_a.shape[1]), lambda b, i: (b * nq + i, 0)),
        out_shape=jax.ShapeDtypeStruct((n_seq * t, q_a.shape[1]), MXU_DTYPE),
        scratch_shapes=[pltpu.VMEM((t, tq), F32), pltpu.VMEM((1, r), F32), pltpu.VMEM((1, r), F32),
                        pltpu.VMEM((hd, r), F32), pltpu.VMEM((tk, r), MXU_DTYPE)],
        compiler_params=_params(("parallel", "arbitrary"), vmem),
        name="dsa_prompt",
    )(q_a, q_i, w_i_t, rows, v_t, bias_t)


def _page_specs(pool, layer, n_group):
    tile = pool.shape[2:]

    def spec(p):
        return pl.BlockSpec((None, None) + tile, lambda b, g, pt: (layer, pt[b, g * n_group + p], 0, 0))

    return [spec(p) for p in range(n_group)]


def _pad_rows(x, n):
    return jnp.concatenate([x, jnp.zeros((n - x.shape[0], x.shape[1]), x.dtype)], axis=0)


def _lane_pair(ref_a, ref_b, lo, hi):
    return jnp.concatenate([ref_a[lo:hi, :], ref_b[lo:hi, :]], axis=1).astype(MXU_DTYPE)


def _mla_sample_kernel(pt_ref, q_ref, new_ref, *rest, n_group, scale, dv):
    pages = rest[:n_group]
    o_ref, kt_ref, m_ref, l_ref, acc_ref = rest[n_group:]
    g = pl.program_id(1)
    n_heads, tq, dk = q_ref.shape
    rows = n_heads * tq
    width = 2 * pages[0].shape[1]
    q = q_ref[...].reshape(rows, dk).astype(MXU_DTYPE)

    @pl.when(g == 0)
    def _():
        _softmax_init(m_ref, l_ref, acc_ref)

    logits = []
    for pp in range(n_group // 2):
        kt = _lane_pair(pages[2 * pp], pages[2 * pp + 1], 0, dk)
        kt_ref[pp] = kt
        logits.append(_dot(q, kt) * scale)
    s = jnp.concatenate(logits, axis=1)
    m_old = m_ref[...]
    m_new = jnp.maximum(m_old, jnp.max(s, axis=1, keepdims=True))
    alpha = jnp.exp(m_old - m_new)
    prob = jnp.exp(s - m_new)
    l_ref[...] = alpha * l_ref[...] + jnp.sum(prob, axis=1, keepdims=True)
    m_ref[...] = m_new
    prob = prob.astype(MXU_DTYPE)
    pv = _nt_dot(prob[:, 0:width], kt_ref[0, 0:dv, :])
    for pp in range(1, n_group // 2):
        pv = pv + _nt_dot(prob[:, pp * width:(pp + 1) * width], kt_ref[pp, 0:dv, :])
    acc_ref[...] = alpha * acc_ref[...] + pv

    @pl.when(g == pl.num_programs(1) - 1)
    def _():
        k = _pad_rows(new_ref[...], LANES).astype(MXU_DTYPE)
        t_row = lax.broadcasted_iota(jnp.int32, (rows, 1), 0) % tq
        j = lax.broadcasted_iota(jnp.int32, (1, LANES), 1)
        s_new = jnp.where(j <= t_row, _nt_dot(q, k) * scale, MASK_VALUE)
        _softmax_step(s_new, k[:, :dv], m_ref, l_ref, acc_ref)
        _rows_to_heads(o_ref, acc_ref[...] / l_ref[...], n_heads, tq, dv)


def mla_sample(q_cat, rows, pool_t, layer, page_table, dv, row0, tq, scale, *, n_group=32):
    n_heads, _, dk = q_cat.shape
    n_seq, n_pages = page_table.shape
    page = pool_t.shape[3]
    n_group = _tile(n_pages, n_group, 2)
    assert n_group % 2 == 0
    qb0 = row0 // tq
    r = n_heads * tq
    grid_spec = pltpu.PrefetchScalarGridSpec(
        num_scalar_prefetch=1,
        grid=(n_seq, n_pages // n_group),
        in_specs=[pl.BlockSpec((n_heads, tq, dk), lambda b, g, pt: (0, qb0 + b, 0)),
                  pl.BlockSpec((tq, dk), lambda b, g, pt: (qb0 + b, 0))] + _page_specs(pool_t, layer, n_group),
        out_specs=pl.BlockSpec((tq, n_heads * dv), lambda b, g, pt: (b, 0)),
        scratch_shapes=[pltpu.VMEM((n_group // 2, dk, 2 * page), MXU_DTYPE),
                        pltpu.VMEM((r, 1), F32), pltpu.VMEM((r, 1), F32), pltpu.VMEM((r, dv), F32)],
    )
    vmem = n_group * dk * page * (2 * 4 + 2) + 4 * r * n_group * page * 4 + (12 << 20)
    return pl.pallas_call(
        functools.partial(_mla_sample_kernel, n_group=n_group, scale=scale, dv=dv),
        grid_spec=grid_spec,
        out_shape=jax.ShapeDtypeStruct((n_seq * tq, n_heads * dv), F32),
        compiler_params=_params(("parallel", "arbitrary"), vmem),
        name="mla_sample",
    )(page_table, q_cat, rows, *([pool_t] * n_group))


def _split3(x):
    hi = x.astype(MXU_DTYPE)
    r1 = x - hi.astype(F32)
    mid = r1.astype(MXU_DTYPE)
    lo = (r1 - mid.astype(F32)).astype(MXU_DTYPE)
    return hi, mid, lo


def _prefix_sum_lanes(blocks):
    c, n = blocks[0].shape
    tri = (lax.broadcasted_iota(jnp.int32, (n, n), 0) <= lax.broadcasted_iota(jnp.int32, (n, n), 1)).astype(MXU_DTYPE)
    pieces = [piece for x in blocks for piece in _split3(x)]
    out = _dot(jnp.concatenate(pieces, axis=0), tri)
    return [out[3 * i * c:(3 * i + 1) * c] + out[(3 * i + 1) * c:(3 * i + 2) * c]
            + out[(3 * i + 2) * c:(3 * i + 3) * c] for i in range(len(blocks))]


def _fox_sample_kernel(pt_ref, q_ref, new_ref, logf_new_ref, *rest, n_pages, scale, hd, batch):
    kv_pages = rest[:n_pages]
    lf_pages = rest[n_pages:2 * n_pages]
    o_ref, cum_ref, s_ref = rest[2 * n_pages:]
    tq = q_ref.shape[0]
    n_heads = q_ref.shape[1] // hd
    rows = n_heads * tq
    page = kv_pages[0].shape[0]
    past = n_pages * page

    carry = jnp.zeros((n_heads, 1), F32)
    for p0 in range(0, n_pages, batch):
        inner = _prefix_sum_lanes([lf_pages[p][...] for p in range(p0, p0 + batch)])
        for p, cum in zip(range(p0, p0 + batch), inner):
            cum_ref[:, p * page:(p + 1) * page] = cum + carry
            carry = carry + cum[:, page - 1:page]
    cum_new = _prefix_sum_lanes([logf_new_ref[...]])[0] + carry
    cum_ref[:, past:past + page] = cum_new
    pick = (lax.broadcasted_iota(jnp.int32, (tq, page), 0)
            == lax.broadcasted_iota(jnp.int32, (tq, page), 1)).astype(MXU_DTYPE)
    cum_q = sum(_nt_dot(pick, piece) for piece in _split3(cum_new))
    cum_q = jnp.stack([cum_q[:, h:h + 1] for h in range(n_heads)])

    q = _heads_to_rows(q_ref[...].astype(MXU_DTYPE), n_heads, hd)

    def logits(raw, cum_k):
        n = raw.shape[1]
        return (raw.reshape(n_heads, tq, n) * scale + cum_q - cum_k[:, None, :]).reshape(rows, n)

    for p in range(0, n_pages, 2):
        k = jnp.concatenate([kv_pages[p][:, 0:hd], kv_pages[p + 1][:, 0:hd]], axis=0).astype(MXU_DTYPE)
        s_ref[:, p * page:(p + 2) * page] = logits(_nt_dot(q, k), cum_ref[:, p * page:(p + 2) * page])
    kv_new = _pad_rows(new_ref[...], page).astype(MXU_DTYPE)
    t_row = lax.broadcasted_iota(jnp.int32, (rows, 1), 0) % tq
    j = lax.broadcasted_iota(jnp.int32, (1, page), 1)
    s_ref[:, past:past + page] = jnp.where(j <= t_row, logits(_nt_dot(q, kv_new[:, :hd]), cum_new), MASK_VALUE)

    s = s_ref[...]
    prob = jnp.exp(s - jnp.max(s, axis=1, keepdims=True))
    denom = jnp.sum(prob, axis=1, keepdims=True)
    prob = prob.astype(MXU_DTYPE)
    out = _dot(prob[:, past:past + page], kv_new[:, hd:])
    for p in range(0, n_pages, 2):
        v = jnp.concatenate([kv_pages[p][:, hd:], kv_pages[p + 1][:, hd:]], axis=0).astype(MXU_DTYPE)
        out = out + _dot(prob[:, p * page:(p + 2) * page], v)
    _rows_to_heads(o_ref, out / denom, n_heads, tq, hd)


def fox_sample(q, kv_new, logf_new_t, kv_pool, logf_pool_t, layer, page_table, row0, tq, scale):
    width = q.shape[1]
    hd = kv_new.shape[1] // 2
    n_heads = width // hd
    n_seq, n_pages = page_table.shape
    page = kv_pool.shape[2]
    assert n_pages % 2 == 0 and page == logf_pool_t.shape[3]
    qb0 = row0 // tq
    r = n_heads * tq
    n_keys = (n_pages + 1) * page
    grid_spec = pltpu.PrefetchScalarGridSpec(
        num_scalar_prefetch=1,
        grid=(n_seq, 1),
        in_specs=[pl.BlockSpec((tq, width), lambda b, g, pt: (qb0 + b, 0)),
                  pl.BlockSpec((tq, 2 * hd), lambda b, g, pt: (qb0 + b, 0)),
                  pl.BlockSpec((None, n_heads, page), lambda b, g, pt: (b, 0, 0))]
        + _page_specs(kv_pool, layer, n_pages) + _page_specs(logf_pool_t, layer, n_pages),
        out_specs=pl.BlockSpec((tq, width), lambda b, g, pt: (b, 0)),
        scratch_shapes=[pltpu.VMEM((n_heads, n_keys), F32), pltpu.VMEM((r, n_keys), F32)],
    )
    vmem = 2 * n_pages * page * (2 * hd + n_heads) * 4 + 4 * r * n_keys * 4 + (8 << 20)
    return pl.pallas_call(
        functools.partial(_fox_sample_kernel, n_pages=n_pages, scale=scale, hd=hd, batch=_tile(n_pages, 8, 1)),
        grid_spec=grid_spec,
        out_shape=jax.ShapeDtypeStruct((n_seq * tq, width), F32),
        compiler_params=_params(("parallel", "arbitrary"), vmem),
        name="fox_sample",
    )(page_table, q, kv_new, logf_new_t, *([kv_pool] * n_pages), *([logf_pool_t] * n_pages))


def _idx_scores(z, wi, n_idx, tq):
    z = jnp.maximum(z, 0.0)
    acc = jnp.zeros((tq, z.shape[1]), F32)
    for h in range(n_idx):
        acc = acc + wi[:, h:h + 1] * z[h * tq:(h + 1) * tq, :]
    return acc


def _dsa_sample_kernel(pt_ref, qa_ref, qi_ref, wi_ref, new_ref, bias_ref, *rest,
                       n_pages, topk, scale, w_scale, hd, idx_dim):
    pages = rest[:n_pages]
    o_ref, keep_ref, s_ref = rest[n_pages:]
    tq = qa_ref.shape[0]
    n_heads = qa_ref.shape[1] // hd
    n_idx = qi_ref.shape[1] // idx_dim
    rows = n_heads * tq
    page = pages[0].shape[1]
    past = n_pages * page
    new = new_ref[...]
    t_row = lax.broadcasted_iota(jnp.int32, (tq, 1), 0)
    j = lax.broadcasted_iota(jnp.int32, (1, page), 1)

    qi = _heads_to_rows(qi_ref[...].astype(MXU_DTYPE), n_idx, idx_dim)
    wi = wi_ref[...] * w_scale
    for p in range(0, n_pages, 2):
        ki_t = _lane_pair(pages[p], pages[p + 1], 2 * hd, 2 * hd + idx_dim)
        keep_ref[:, p * page:(p + 2) * page] = _idx_scores(_dot(qi, ki_t), wi, n_idx, tq)
    ki_new = _pad_rows(new[:, 2 * hd:2 * hd + idx_dim], page).astype(MXU_DTYPE)
    keep_ref[:, past:past + page] = jnp.where(j <= t_row, _idx_scores(_nt_dot(qi, ki_new), wi, n_idx, tq), -jnp.inf)

    score = keep_ref[...]
    key = _order_key(score)
    thr = _kth_largest_key(key, topk)
    keep_ref[...] = jnp.where((key >= thr) & (score > -jnp.inf), 0.0, MASK_VALUE)

    q = _heads_to_rows(qa_ref[...].astype(MXU_DTYPE), n_heads, hd)

    def logits(raw, bias, keep):
        n = raw.shape[1]
        return (raw.reshape(n_heads, tq, n) * scale + bias + keep[None]).reshape(rows, n)

    far = bias_ref[2]
    for p in range(0, n_pages, 2):
        k_t = _lane_pair(pages[p], pages[p + 1], 0, hd)
        bias = jnp.concatenate([far, bias_ref[1] if p + 2 == n_pages else far], axis=2)
        s_ref[:, p * page:(p + 2) * page] = logits(_dot(q, k_t), bias, keep_ref[:, p * page:(p + 2) * page])
    kv_new = _pad_rows(new[:, 0:2 * hd], page).astype(MXU_DTYPE)
    s_ref[:, past:past + page] = logits(_nt_dot(q, kv_new[:, :hd]), bias_ref[0], keep_ref[:, past:past + page])

    s = s_ref[...]
    prob = jnp.exp(s - jnp.max(s, axis=1, keepdims=True))
    denom = jnp.sum(prob, axis=1, keepdims=True)
    prob = prob.astype(MXU_DTYPE)
    out = _dot(prob[:, past:past + page], kv_new[:, hd:])
    for p in range(0, n_pages, 2):
        out = out + _nt_dot(prob[:, p * page:(p + 2) * page], _lane_pair(pages[p], pages[p + 1], hd, 2 * hd))
    _rows_to_heads(o_ref, out / denom, n_heads, tq, hd)


def dsa_sample(q_a, q_i, w_i, rows, bias, pool_t, layer, page_table, row0, tq, scale, w_scale, hd, idx_dim):
    n_seq, n_pages = page_table.shape
    page = pool_t.shape[3]
    assert page == LANES and n_pages % 2 == 0
    qb0 = row0 // tq
    width = q_a.shape[1]
    n_keys = (n_pages + 1) * page
    r = (width // hd) * tq
    grid_spec = pltpu.PrefetchScalarGridSpec(
        num_scalar_prefetch=1,
        grid=(n_seq, 1),
        in_specs=[pl.BlockSpec((tq, width), lambda b, g, pt: (qb0 + b, 0)),
                  pl.BlockSpec((tq, q_i.shape[1]), lambda b, g, pt: (qb0 + b, 0)),
                  pl.BlockSpec((tq, w_i.shape[1]), lambda b, g, pt: (qb0 + b, 0)),
                  pl.BlockSpec((tq, rows.shape[1]), lambda b, g, pt: (qb0 + b, 0)),
                  pl.BlockSpec(bias.shape, lambda b, g, pt: (0, 0, 0, 0))] + _page_specs(pool_t, layer, n_pages),
        out_specs=pl.BlockSpec((tq, width), lambda b, g, pt: (b, 0)),
        scratch_shapes=[pltpu.VMEM((tq, n_keys), F32), pltpu.VMEM((r, n_keys), F32)],
    )
    vmem = 2 * n_pages * pool_t.shape[2] * page * 4 + 4 * r * n_keys * 4 + (10 << 20)
    return pl.pallas_call(
        functools.partial(_dsa_sample_kernel, n_pages=n_pages, topk=min(DSA_TOPK, (n_pages * page + tq) // 4),
                          scale=scale, w_scale=w_scale, hd=hd, idx_dim=idx_dim),
        grid_spec=grid_spec,
        out_shape=jax.ShapeDtypeStruct((n_seq * tq, width), F32),
        compiler_params=_params(("parallel", "arbitrary"), vmem),
        name="dsa_sample",
    )(page_table, q_a, q_i, w_i, rows, bias, *([pool_t] * n_pages))


def kernel(x_prompt, x_sample, cache_dsa, cache_mla, cache_fox_kv, cache_fox_logf, page_table, p_prompt, p_sample,
           rel_bias, w_in_ab, mla_q_norm, w_uq, mla_kv_norm, w_uk, w_uv, w_o_ab, w_in_c, b_forget, w_o_c,
           ln1_g, ln1_b, ln2_g, ln2_b, w_ffn_up, w_ffn_down, w_router, w_moe_up, w_moe_down,
           w_ple_gate, b_ple_gate, w_ple_proj):
    n_p, t_p, d = x_prompt.shape
    n_s, t_s, _ = x_sample.shape
    depth = ln1_g.shape[0]
    m_p, m_s = n_p * t_p, n_s * t_s
    hd = cache_fox_kv.shape[-1] // 2
    idx_dim = cache_dsa.shape[-1] - 2 * hd
    kv_lora = w_uk.shape[1]
    rope = cache_mla.shape[-1] - kv_lora
    q_lora = w_uq.shape[1]
    nope = w_uk.shape[3]
    h_a = rel_bias.shape[1]
    h_c = b_forget.shape[1]
    ab_cols = w_in_ab.shape[2]
    n_idx_w = (ab_cols - h_a * hd - 2 * hd - idx_dim - q_lora - kv_lora - rope) // (idx_dim + 1)
    past_len = page_table.shape[1] * cache_dsa.shape[2]
    alpha = (2 * depth) ** 0.25
    attn_scale = hd ** -0.5
    mla_scale = (nope + rope) ** -0.5
    idx_w_scale = (n_idx_w * idx_dim) ** -0.5

    c_qa = 0
    c_kv = h_a * hd
    c_qi = c_kv + 2 * hd
    c_tail = c_qi + n_idx_w * idx_dim
    t_wi = idx_dim
    t_cq = t_wi + n_idx_w
    t_ckv = t_cq + q_lora
    t_kr = t_ckv + kv_lora

    x = jnp.concatenate([x_prompt.reshape(m_p, d), x_sample.reshape(m_s, d)], axis=0)
    xb = x.astype(MXU_DTYPE)
    ple = jnp.concatenate([p_prompt.reshape(depth, m_p, -1), p_sample.reshape(depth, m_s, -1)], axis=1)
    pos = jnp.concatenate([jnp.tile(jnp.arange(t_p, dtype=jnp.int32), n_p),
                           past_len + jnp.tile(jnp.arange(t_s, dtype=jnp.int32), n_s)])
    inv = ROPE_THETA ** (-jnp.arange(0, rope, 2, dtype=F32) / rope)
    ang = pos.astype(F32)[:, None] * inv[None, :]
    cos, sin = jnp.cos(ang), jnp.sin(ang)
    bias_p_t = jnp.swapaxes(bias_tables(rel_bias, LANES), 2, 3)
    bias_s = bias_tables(rel_bias, t_s)
    dsa_pool_t = jnp.swapaxes(cache_dsa, 2, 3)
    mla_pool_t = jnp.swapaxes(cache_mla, 2, 3)
    logf_pool_t = jnp.swapaxes(cache_fox_logf, 2, 3)

    new_dsa, new_mla, new_fkv, new_flf = [], [], [], []
    for i in range(depth):
        j = i // 2
        if i % 2 == 0:
            w_in = w_in_ab[j]
            q_a = matmul(xb, w_in, out_dtype=F32, col_start=c_qa, n_cols=h_a * hd)
            kv_a = matmul(xb, w_in, out_dtype=F32, col_start=c_kv, n_cols=2 * hd)
            q_i = matmul(xb, w_in, out_dtype=F32, col_start=c_qi, n_cols=n_idx_w * idx_dim)
            tail = matmul(xb, w_in[:, c_tail:], out_dtype=F32)
            w_i = tail[:, t_wi:t_cq]
            cqn, mla_rows = ab_post(tail[:, t_cq:t_ckv], tail[:, t_ckv:t_kr], tail[:, t_kr:], cos, sin,
                                    mla_q_norm[j], mla_kv_norm[j])
            dsa_rows = jnp.concatenate([kv_a, tail[:, :t_wi]], axis=-1)
            q_cat = mla_queries(cqn, w_uq[j], w_uk[j], cos, sin)
            w_uv_heads = jnp.transpose(w_uv[j], (1, 0, 2))

            o_a_p = dsa_prompt(q_a, q_i, jnp.transpose(w_i), dsa_rows, jnp.transpose(kv_a[:, hd:]), bias_p_t,
                               n_p, t_p, attn_scale, idx_w_scale, hd, idx_dim)
            o_a_s = dsa_sample(q_a, q_i, w_i, dsa_rows, bias_s, dsa_pool_t, j, page_table, m_p, t_s,
                               attn_scale, idx_w_scale, hd, idx_dim)
            o_b_p = mla_out(mla_prompt(q_cat, mla_rows, jnp.transpose(mla_rows[:, :kv_lora]), n_p, t_p, mla_scale),
                            w_uv_heads)
            o_b_s = mla_out(mla_sample(q_cat, mla_rows, mla_pool_t, j, page_table, kv_lora, m_p, t_s, mla_scale),
                            w_uv_heads)
            heads = jnp.concatenate([jnp.concatenate([o_a_p, o_a_s.astype(MXU_DTYPE)], axis=0),
                                     jnp.concatenate([o_b_p, o_b_s], axis=0)], axis=1)
            mix = matmul(heads, w_o_ab[j], out_dtype=F32)
            new_dsa.append(dsa_rows)
            new_mla.append(mla_rows)
        else:
            w_in = w_in_c[j]
            q = matmul(xb, w_in, out_dtype=F32, col_start=0, n_cols=h_c * hd)
            kv_c = matmul(xb, w_in, out_dtype=F32, col_start=h_c * hd, n_cols=2 * hd)
            logf = fox_log_forget(xb, w_in[:, h_c * hd + 2 * hd:], b_forget[j])
            cum_p = cumsum_prompt(logf[:m_p], n_p, t_p)
            cum_t_p = jnp.swapaxes(cum_p.reshape(n_p, t_p, h_c), 1, 2)
            o_p = fox_prompt(q, kv_c, jnp.transpose(kv_c[:, hd:]), cum_p, cum_t_p, n_p, t_p, attn_scale)
            logf_new_t = jnp.pad(jnp.swapaxes(logf[m_p:].reshape(n_s, t_s, h_c), 1, 2),
                                 ((0, 0), (0, 0), (0, logf_pool_t.shape[3] - t_s)))
            o_s = fox_sample(q, kv_c, logf_new_t, cache_fox_kv, logf_pool_t, j, page_table, m_p, t_s, attn_scale)
            heads = jnp.concatenate([o_p, o_s.astype(MXU_DTYPE)], axis=0)
            mix = matmul(heads, w_o_c[j], out_dtype=F32)
            new_fkv.append(kv_c)
            new_flf.append(logf)
        h, hb = deepnorm(x, mix, ln1_g[i], ln1_b[i], alpha)
        if i % 2 == 0:
            ff = matmul(swiglu_up(hb, w_ffn_up[j]), w_ffn_down[j], out_dtype=F32, tk=w_ffn_down.shape[1] // 2)
        else:
            combine = router_combine(hb, w_router[j])
            ff = moe_down(moe_up(hb, w_moe_up[j]), w_moe_down[j], jnp.transpose(combine)[:, :, None])
        h, hb = deepnorm(h, ff, ln2_g[i], ln2_b[i], alpha)
        x, xb = ple_mix(h, hb, w_ple_gate[i], b_ple_gate[i], ple[i], w_ple_proj[i])

    def split(rows_list):
        a = jnp.stack(rows_list)
        return a[:, :m_p].reshape(len(rows_list), n_p, t_p, -1), a[:, m_p:].reshape(len(rows_list), n_s, t_s, -1)

    dsa_p, dsa_s = split(new_dsa)
    mla_p, mla_s = split(new_mla)
    fkv_p, fkv_s = split(new_fkv)
    flf_p, flf_s = split(new_flf)
    return (x[:m_p].reshape(n_p, t_p, d), x[m_p:].reshape(n_s, t_s, d),
            dsa_p, mla_p, fkv_p, flf_p, dsa_s, mla_s, fkv_s, flf_s)
```

```python
import functools
import math

import numpy as np
import jax
import jax.numpy as jnp
from jax import lax
from jax.experimental import pallas as pl
from jax.experimental.pallas import tpu as pltpu

DSA_TOPK = 256
N_BUCKETS = 32
MAX_DISTANCE = 128
ROPE_THETA = 10000.0
LN_EPS = 1e-5
RMS_EPS = 1e-6
MOE_TOP_K = 2

MXU_DTYPE = jnp.bfloat16
MASK_VALUE = -1e30
V7X_VMEM_BYTES = 64 << 20
VMEM_CAP_BYTES = V7X_VMEM_BYTES - (6 << 20)
LANES = 128
F32 = jnp.float32


def _tile(n, target, align=8):
    if n <= target:
        return n
    for d in range(target, 0, -1):
        if n % d == 0 and d % align == 0:
            return d
    return n


def _params(semantics, vmem_bytes):
    limit = int(min(max(vmem_bytes, 16 << 20), VMEM_CAP_BYTES))
    return pltpu.CompilerParams(dimension_semantics=semantics, vmem_limit_bytes=limit)


def _nt_dot(a, b):
    return lax.dot_general(a, b, (((1,), (1,)), ((), ())), preferred_element_type=F32)


def _dot(a, b):
    return jnp.dot(a, b, preferred_element_type=F32)


def _mm_kernel(x_ref, w_ref, o_ref, *scratch, nk):
    part = _dot(x_ref[...].astype(MXU_DTYPE), w_ref[...].astype(MXU_DTYPE))
    if nk == 1:
        o_ref[...] = part.astype(o_ref.dtype)
        return
    acc_ref, = scratch
    k = pl.program_id(2)

    @pl.when(k == 0)
    def _():
        acc_ref[...] = part

    @pl.when(k > 0)
    def _():
        acc_ref[...] += part

    @pl.when(k == nk - 1)
    def _():
        o_ref[...] = acc_ref[...].astype(o_ref.dtype)


def matmul(x, w, *, out_dtype, layer=None, col_start=0, n_cols=None, tm=1024, tn=256, tk=8192):
    m, kdim = x.shape
    n_total = w.shape[-1]
    n_cols = n_total - col_start if n_cols is None else n_cols
    tm = _tile(m, tm)
    tk = _tile(kdim, tk, LANES)
    if n_cols % LANES == 0 and col_start % LANES == 0:
        tn = _tile(n_cols, tn, LANES)
        while col_start % tn:
            tn -= LANES
    else:
        assert col_start == 0 and n_cols == n_total
        tn = n_cols
    off = col_start // tn
    xb, ob = x.dtype.itemsize, jnp.dtype(out_dtype).itemsize
    cast_x = 0 if x.dtype == MXU_DTYPE else 2

    tn_pad = -(-tn // LANES) * LANES

    def estimate(tk):
        return (2 * (tm * tk * xb + tk * tn_pad * 4 + tm * tn_pad * ob) + 2 * tm * tn_pad * 4 + tk * tn_pad * 2
                + tm * tk * cast_x + (4 << 20))

    while estimate(tk) > VMEM_CAP_BYTES * 7 // 8 and tk % (2 * LANES) == 0:
        tk //= 2
    nk = kdim // tk
    vmem = estimate(tk)
    if layer is None:
        w_spec = pl.BlockSpec((tk, tn), lambda i, j, k: (k, j + off))
    else:
        w_spec = pl.BlockSpec((None, tk, tn), lambda i, j, k: (layer, k, j + off))
    return pl.pallas_call(
        functools.partial(_mm_kernel, nk=nk),
        grid=(m // tm, n_cols // tn, nk),
        in_specs=[pl.BlockSpec((tm, tk), lambda i, j, k: (i, k)), w_spec],
        out_specs=pl.BlockSpec((tm, tn), lambda i, j, k: (i, j)),
        out_shape=jax.ShapeDtypeStruct((m, n_cols), out_dtype),
        scratch_shapes=[pltpu.VMEM((tm, tn), F32)] if nk > 1 else [],
        compiler_params=_params(("parallel", "parallel", "arbitrary"), vmem),
        name="matmul",
    )(x, w)


def _swiglu_up_kernel(x_ref, wg_ref, wu_ref, o_ref):
    x = x_ref[...]
    gate = _dot(x, wg_ref[...].astype(MXU_DTYPE))
    up = _dot(x, wu_ref[...].astype(MXU_DTYPE))
    o_ref[...] = (gate * jax.nn.sigmoid(gate) * up).astype(o_ref.dtype)


def swiglu_up(xb, w_up, layer, *, tm=1024, tn=256):
    m, d = xb.shape
    f = w_up.shape[2] // 2
    tm, tn = _tile(m, tm), _tile(f, tn, LANES)
    nf = f // tn
    vmem = 2 * (tm * d * 2 + 2 * d * tn * 4 + tm * tn * 2) + 2 * d * tn * 2 + 3 * tm * tn * 4 + (4 << 20)
    return pl.pallas_call(
        _swiglu_up_kernel,
        grid=(m // tm, nf),
        in_specs=[pl.BlockSpec((tm, d), lambda i, j: (i, 0)),
                  pl.BlockSpec((None, d, tn), lambda i, j: (layer, 0, j)),
                  pl.BlockSpec((None, d, tn), lambda i, j: (layer, 0, j + nf))],
        out_specs=pl.BlockSpec((tm, tn), lambda i, j: (i, j)),
        out_shape=jax.ShapeDtypeStruct((m, f), MXU_DTYPE),
        compiler_params=_params(("parallel", "parallel"), vmem),
        name="swiglu_up",
    )(xb, w_up, w_up)


def moe_up(xb, w_up, layer, *, tm=1024, tn=256):
    m, d = xb.shape
    _, n_exp, _, f2 = w_up.shape
    f = f2 // 2
    tm, tn = _tile(m, tm), _tile(f, tn, LANES)
    nf = f // tn
    vmem = 2 * (tm * d * 2 + 2 * d * tn * 4 + tm * tn * 2) + 2 * d * tn * 2 + 3 * tm * tn * 4 + (4 << 20)
    return pl.pallas_call(
        _swiglu_up_kernel,
        grid=(m // tm, n_exp, nf),
        in_specs=[pl.BlockSpec((tm, d), lambda i, e, j: (i, 0)),
                  pl.BlockSpec((None, None, d, tn), lambda i, e, j: (layer, e, 0, j)),
                  pl.BlockSpec((None, None, d, tn), lambda i, e, j: (layer, e, 0, j + nf))],
        out_specs=pl.BlockSpec((tm, tn), lambda i, e, j: (i, e * nf + j)),
        out_shape=jax.ShapeDtypeStruct((m, n_exp * f), MXU_DTYPE),
        compiler_params=_params(("parallel", "parallel", "parallel"), vmem),
        name="moe_up",
    )(xb, w_up, w_up)


def _moe_down_kernel(a_ref, w_ref, c_ref, o_ref, acc_ref, *, n_exp):
    e = pl.program_id(2)
    part = c_ref[...] * _dot(a_ref[...], w_ref[...].astype(MXU_DTYPE))

    @pl.when(e == 0)
    def _():
        acc_ref[...] = part

    @pl.when(e > 0)
    def _():
        acc_ref[...] += part

    @pl.when(e == n_exp - 1)
    def _():
        o_ref[...] = acc_ref[...]


def moe_down(act, w_down, layer, combine_t, *, tm=1024, tn=512):
    m = act.shape[0]
    _, n_exp, f, d = w_down.shape
    tm, tn = _tile(m, tm), _tile(d, tn, LANES)
    vmem = 2 * (tm * f * 2 + f * tn * 4 + tm * LANES * 4 + tm * tn * 4) + 3 * tm * tn * 4 + f * tn * 2 + (4 << 20)
    return pl.pallas_call(
        functools.partial(_moe_down_kernel, n_exp=n_exp),
        grid=(m // tm, d // tn, n_exp),
        in_specs=[pl.BlockSpec((tm, f), lambda i, j, e: (i, e)),
                  pl.BlockSpec((None, None, f, tn), lambda i, j, e: (layer, e, 0, j)),
                  pl.BlockSpec((None, tm, 1), lambda i, j, e: (e, i, 0))],
        out_specs=pl.BlockSpec((tm, tn), lambda i, j, e: (i, j)),
        out_shape=jax.ShapeDtypeStruct((m, d), F32),
        scratch_shapes=[pltpu.VMEM((tm, tn), F32)],
        compiler_params=_params(("parallel", "parallel", "arbitrary"), vmem),
        name="moe_down",
    )(act, w_down, combine_t)


def _router_kernel(x_ref, w_ref, o_ref):
    logits = _dot(x_ref[...], w_ref[...].astype(MXU_DTYPE))
    n_exp = logits.shape[1]
    lane = lax.broadcasted_iota(jnp.int32, logits.shape, 1)
    m1 = jnp.max(logits, axis=1, keepdims=True)
    i1 = jnp.min(jnp.where(logits == m1, lane, n_exp), axis=1, keepdims=True)
    rest = jnp.where(lane == i1, -jnp.inf, logits)
    m2 = jnp.max(rest, axis=1, keepdims=True)
    i2 = jnp.min(jnp.where(rest == m2, lane, n_exp), axis=1, keepdims=True)
    e2 = jnp.exp(m2 - m1)
    denom = 1.0 + e2
    o_ref[...] = jnp.where(lane == i1, 1.0 / denom, 0.0) + jnp.where(lane == i2, e2 / denom, 0.0)


def router_combine(xb, w_router, *, tm=1024):
    m, d = xb.shape
    n_exp = w_router.shape[1]
    tm = _tile(m, tm)
    vmem = 2 * (tm * d * 2 + d * LANES * 4 + tm * LANES * 4) + (8 << 20)
    return pl.pallas_call(
        _router_kernel,
        grid=(m // tm,),
        in_specs=[pl.BlockSpec((tm, d), lambda i: (i, 0)),
                  pl.BlockSpec((d, n_exp), lambda i: (0, 0))],
        out_specs=pl.BlockSpec((tm, n_exp), lambda i: (i, 0)),
        out_shape=jax.ShapeDtypeStruct((m, n_exp), F32),
        compiler_params=_params(("parallel",), vmem),
    )(xb, w_router)


def _deepnorm_kernel(x_ref, y_ref, g_ref, b_ref, o_ref, ob_ref, *, alpha):
    z = alpha * x_ref[...] + y_ref[...]
    mu = jnp.mean(z, axis=-1, keepdims=True)
    zc = z - mu
    var = jnp.mean(zc * zc, axis=-1, keepdims=True)
    out = zc * lax.rsqrt(var + LN_EPS) * g_ref[...] + b_ref[...]
    o_ref[...] = out
    ob_ref[...] = out.astype(ob_ref.dtype)


def deepnorm(x, y, g, b, alpha, *, tm=256):
    m, d = x.shape
    tm = _tile(m, tm)
    row = pl.BlockSpec((tm, d), lambda i: (i, 0))
    vec = pl.BlockSpec((1, d), lambda i: (0, 0))
    return pl.pallas_call(
        functools.partial(_deepnorm_kernel, alpha=alpha),
        grid=(m // tm,),
        in_specs=[row, row, vec, vec],
        out_specs=[row, row],
        out_shape=[jax.ShapeDtypeStruct((m, d), F32), jax.ShapeDtypeStruct((m, d), MXU_DTYPE)],
        compiler_params=_params(("parallel",), 12 * tm * d * 4 + (4 << 20)),
    )(x, y, g.reshape(1, d), b.reshape(1, d))


def _ple_kernel(hb_ref, wg_ref, bg_ref, p_ref, wp_ref, h_ref, o_ref, ob_ref):
    gate = jax.nn.sigmoid(_dot(hb_ref[...], wg_ref[...].astype(MXU_DTYPE)) + bg_ref[...])
    emb = _dot(p_ref[...].astype(MXU_DTYPE), wp_ref[...].astype(MXU_DTYPE))
    out = h_ref[...] + gate * emb
    o_ref[...] = out
    ob_ref[...] = out.astype(ob_ref.dtype)


def ple_mix(h, hb, w_gate, b_gate, p, w_proj, layer, *, tm=1024, tn=256):
    m, d = h.shape
    pd = p.shape[2]
    tm, tn = _tile(m, tm), _tile(d, tn, LANES)
    vmem = 2 * (tm * d * 2 + d * tn * 4 + tm * pd * 4 + pd * tn * 4 + 3 * tm * tn * 4) + d * tn * 2 + 4 * tm * tn * 4 + (4 << 20)
    tile = pl.BlockSpec((tm, tn), lambda i, j: (i, j))
    return pl.pallas_call(
        _ple_kernel,
        grid=(m // tm, d // tn),
        in_specs=[pl.BlockSpec((tm, d), lambda i, j: (i, 0)),
                  pl.BlockSpec((None, d, tn), lambda i, j: (layer, 0, j)),
                  pl.BlockSpec((1, tn), lambda i, j: (0, j)),
                  pl.BlockSpec((None, tm, pd), lambda i, j: (layer, i, 0)),
                  pl.BlockSpec((None, pd, tn), lambda i, j: (layer, 0, j)),
                  tile],
        out_specs=[tile, tile],
        out_shape=[jax.ShapeDtypeStruct((m, d), F32), jax.ShapeDtypeStruct((m, d), MXU_DTYPE)],
        compiler_params=_params(("parallel", "parallel"), vmem),
        name="ple_mix",
    )(hb, w_gate, b_gate.reshape(1, d), p, w_proj, h)


def _rms(x, g):
    return x * lax.rsqrt(jnp.mean(x * x, axis=-1, keepdims=True) + RMS_EPS) * g


def _ab_post_kernel(cq_ref, ckv_ref, k1_ref, k2_ref, cos_ref, sin_ref, gq_ref, gkv_ref,
                    cqn_ref, ckvn_ref, r1_ref, r2_ref):
    cqn_ref[...] = _rms(cq_ref[...], gq_ref[...]).astype(cqn_ref.dtype)
    ckvn_ref[...] = _rms(ckv_ref[...], gkv_ref[...])
    x1, x2, cos, sin = k1_ref[...], k2_ref[...], cos_ref[...], sin_ref[...]
    r1_ref[...] = x1 * cos - x2 * sin
    r2_ref[...] = x1 * sin + x2 * cos


def ab_post(c_q, c_kv, k_r, cos, sin, q_norm, kv_norm, *, tm=512):
    m, ql = c_q.shape
    kl, half = c_kv.shape[1], cos.shape[1]
    tm = _tile(m, tm)

    def row(n):
        return pl.BlockSpec((tm, n), lambda i: (i, 0))

    def vec(n):
        return pl.BlockSpec((1, n), lambda i: (0, 0))

    cqn, ckvn, r1, r2 = pl.pallas_call(
        _ab_post_kernel,
        grid=(m // tm,),
        in_specs=[row(ql), row(kl), row(half), row(half), row(half), row(half), vec(ql), vec(kl)],
        out_specs=[row(ql), row(kl), row(half), row(half)],
        out_shape=[jax.ShapeDtypeStruct((m, ql), MXU_DTYPE), jax.ShapeDtypeStruct((m, kl), F32),
                   jax.ShapeDtypeStruct((m, half), F32), jax.ShapeDtypeStruct((m, half), F32)],
        compiler_params=_params(("parallel",), 32 << 20),
    )(c_q, c_kv, k_r[:, :half], k_r[:, half:], cos, sin, q_norm.reshape(1, ql), kv_norm.reshape(1, kl))
    return cqn, jnp.concatenate([ckvn, r1, r2], axis=-1)


def _mla_q_kernel(c_ref, wn_ref, w1_ref, w2_ref, wk_ref, cos_ref, sin_ref, o_ref, *, kl, half):
    c = c_ref[...]
    q_nope = _dot(c, wn_ref[...].astype(MXU_DTYPE))
    x1 = _dot(c, w1_ref[...].astype(MXU_DTYPE))
    x2 = _dot(c, w2_ref[...].astype(MXU_DTYPE))
    cos, sin = cos_ref[...], sin_ref[...]
    o_ref[:, 0:kl] = _nt_dot(q_nope.astype(MXU_DTYPE), wk_ref[...].astype(MXU_DTYPE))
    o_ref[:, kl:kl + half] = x1 * cos - x2 * sin
    o_ref[:, kl + half:kl + 2 * half] = x1 * sin + x2 * cos


def mla_queries(cqn, w_uq, w_uk, cos, sin, *, tm=1024):
    m, ql = cqn.shape
    kl, n_heads, nope = w_uk.shape
    half = cos.shape[1]
    tm = _tile(m, tm)
    w_heads = jnp.transpose(w_uq, (1, 0, 2))
    wk_heads = jnp.transpose(w_uk, (1, 0, 2))
    return pl.pallas_call(
        functools.partial(_mla_q_kernel, kl=kl, half=half),
        grid=(m // tm, n_heads),
        in_specs=[pl.BlockSpec((tm, ql), lambda i, h: (i, 0)),
                  pl.BlockSpec((None, ql, nope), lambda i, h: (h, 0, 0)),
                  pl.BlockSpec((None, ql, half), lambda i, h: (h, 0, 0)),
                  pl.BlockSpec((None, ql, half), lambda i, h: (h, 0, 0)),
                  pl.BlockSpec((None, kl, nope), lambda i, h: (h, 0, 0)),
                  pl.BlockSpec((tm, half), lambda i, h: (i, 0)),
                  pl.BlockSpec((tm, half), lambda i, h: (i, 0))],
        out_specs=pl.BlockSpec((None, tm, kl + 2 * half), lambda i, h: (h, i, 0)),
        out_shape=jax.ShapeDtypeStruct((n_heads, m, kl + 2 * half), F32),
        compiler_params=_params(("parallel", "parallel"), 32 << 20),
    )(cqn, w_heads[:, :, :nope], w_heads[:, :, nope:nope + half], w_heads[:, :, nope + half:], wk_heads, cos, sin)


def _fox_gate_kernel(x_ref, w_ref, b_ref, o_ref):
    z = _dot(x_ref[...], w_ref[...].astype(MXU_DTYPE)) + b_ref[...]
    o_ref[...] = jnp.minimum(z, 0.0) - jnp.log1p(jnp.exp(-jnp.abs(z)))


def fox_log_forget(xb, w_f, b_f, *, tm=1024):
    m, d = xb.shape
    h = w_f.shape[1]
    tm = _tile(m, tm)
    return pl.pallas_call(
        _fox_gate_kernel,
        grid=(m // tm,),
        in_specs=[pl.BlockSpec((tm, d), lambda i: (i, 0)),
                  pl.BlockSpec((d, h), lambda i: (0, 0)),
                  pl.BlockSpec((1, h), lambda i: (0, 0))],
        out_specs=pl.BlockSpec((tm, h), lambda i: (i, 0)),
        out_shape=jax.ShapeDtypeStruct((m, h), F32),
        compiler_params=_params(("parallel",), 2 * (tm * d * 2 + d * LANES * 4 + tm * LANES * 4) + (8 << 20)),
    )(xb, w_f, b_f.reshape(1, h))


def _softmax_init(m_ref, l_ref, acc_ref):
    m_ref[...] = jnp.full(m_ref.shape, MASK_VALUE, F32)
    l_ref[...] = jnp.zeros(l_ref.shape, F32)
    acc_ref[...] = jnp.zeros(acc_ref.shape, F32)


def _softmax_step(s, v, m_ref, l_ref, acc_ref):
    m_old = m_ref[...]
    m_new = jnp.maximum(m_old, jnp.max(s, axis=1, keepdims=True))
    alpha = jnp.exp(m_old - m_new)
    p = jnp.exp(s - m_new)
    l_ref[...] = alpha * l_ref[...] + jnp.sum(p, axis=1, keepdims=True)
    acc_ref[...] = alpha * acc_ref[...] + _dot(p.astype(MXU_DTYPE), v)
    m_ref[...] = m_new


def _softmax_step_cols(logits_of_head, v_t, n_heads, tq, m_ref, l_ref, acc_ref, p_ref):
    for h in range(n_heads):
        c = slice(h * tq, (h + 1) * tq)
        s = logits_of_head(h)
        m_old = m_ref[:, c]
        m_new = jnp.maximum(m_old, jnp.max(s, axis=0, keepdims=True))
        alpha = jnp.exp(m_old - m_new)
        p = jnp.exp(s - m_new)
        l_ref[:, c] = alpha * l_ref[:, c] + jnp.sum(p, axis=0, keepdims=True)
        m_ref[:, c] = m_new
        acc_ref[:, c] = alpha * acc_ref[:, c]
        p_ref[:, c] = p.astype(p_ref.dtype)
    acc_ref[...] += _dot(v_t, p_ref[...])


def _cols_to_heads(o_ref, o_t, n_heads, tq, dv):
    for h in range(n_heads):
        o_ref[:, h * dv:(h + 1) * dv] = o_t[:, h * tq:(h + 1) * tq].T.astype(o_ref.dtype)


def _heads_to_rows(q, n_heads, hd):
    return jnp.concatenate([q[:, h * hd:(h + 1) * hd] for h in range(n_heads)], axis=0)


def _rows_to_heads(o_ref, o, n_heads, t, hd):
    for h in range(n_heads):
        o_ref[:, h * hd:(h + 1) * hd] = o[h * t:(h + 1) * t, :].astype(o_ref.dtype)


def _prefix_sum_rows(x, carry):
    n = x.shape[0]
    tri = (lax.broadcasted_iota(jnp.int32, (n, n), 1) <= lax.broadcasted_iota(jnp.int32, (n, n), 0)).astype(MXU_DTYPE)
    hi = x.astype(MXU_DTYPE)
    r1 = x - hi.astype(F32)
    mid = r1.astype(MXU_DTYPE)
    lo = (r1 - mid.astype(F32)).astype(MXU_DTYPE)
    return _dot(tri, hi) + _dot(tri, mid) + _dot(tri, lo) + carry


def _order_key(score):
    bits = lax.bitcast_convert_type(score, jnp.int32)
    return jnp.where(bits < 0, bits ^ jnp.int32(0x7FFFFFFF), bits)


def _kth_largest_key(key, k, axis=1):
    shape = tuple(1 if a == axis else n for a, n in enumerate(key.shape))

    step = 8 if axis == 0 else LANES

    def count_ge(t):
        hit = jnp.where(key >= t, 1.0, 0.0)
        parts = [lax.slice_in_dim(hit, s, s + step, axis=axis) for s in range(0, key.shape[axis], step)]
        while len(parts) > 1:
            parts = [a + b for a, b in zip(parts[0::2], parts[1::2])] + parts[len(parts) - len(parts) % 2:]
        return jnp.sum(parts[0], axis=axis, keepdims=True)

    int_min = jnp.int32(-2 ** 31)
    thr = jnp.where(count_ge(jnp.zeros(shape, jnp.int32)) >= k, jnp.int32(0), int_min)

    def body(it, thr):
        cand = thr | lax.shift_left(jnp.int32(1), jnp.int32(30) - it)
        return jnp.where(count_ge(cand) >= k, cand, thr)

    return lax.fori_loop(0, 31, body, thr)


def bias_tables(rel_bias, tq):
    max_exact = N_BUCKETS // 2
    d = np.arange(0, 2 * LANES + tq)
    ratio = np.log(np.maximum(d, 1).astype(np.float32) / max_exact) / math.log(MAX_DISTANCE / max_exact)
    large = np.minimum(max_exact + (ratio * (N_BUCKETS - max_exact)).astype(np.int32), N_BUCKETS - 1)
    bucket = np.where(d < max_exact, d, large)
    i = np.arange(tq)[:, None]
    j = np.arange(LANES)[None, :]
    dist = np.stack([np.maximum(i - j, 0), i - j + LANES, np.full((tq, LANES), 2 * LANES)])
    one_hot = (bucket[dist][..., None] == np.arange(N_BUCKETS)).astype(np.float32)
    return jnp.einsum("abcn,nh->ahbc", one_hot, rel_bias.astype(F32), precision=lax.Precision.HIGHEST)


def _mla_out_kernel(o_ref, w_ref, out_ref):
    out_ref[...] = _dot(o_ref[...].astype(MXU_DTYPE), w_ref[...].astype(MXU_DTYPE)).astype(out_ref.dtype)


def mla_out(o_lat, w_uv_heads, *, tm=1024):
    m = o_lat.shape[0]
    n_heads, dv, hd = w_uv_heads.shape
    tm = _tile(m, tm)
    return pl.pallas_call(
        _mla_out_kernel,
        grid=(m // tm, n_heads),
        in_specs=[pl.BlockSpec((tm, dv), lambda i, h: (i, h)),
                  pl.BlockSpec((None, dv, hd), lambda i, h: (h, 0, 0))],
        out_specs=pl.BlockSpec((tm, hd), lambda i, h: (i, h)),
        out_shape=jax.ShapeDtypeStruct((m, n_heads * hd), MXU_DTYPE),
        compiler_params=_params(("parallel", "parallel"), 24 << 20),
        name="mla_out",
    )(o_lat, w_uv_heads)


def _causal_chunks(i, tq, tk, step):
    n_full = (i * tq + 1) // tk

    def body(kb, carry):
        step(kb, False)
        return carry

    lax.fori_loop(0, n_full, body, 0)
    step(n_full, True)


def _key_visible(kb, tk, i, tq):
    k_pos = kb * tk + lax.broadcasted_iota(jnp.int32, (tk, 1), 0)
    q_pos = i * tq + lax.broadcasted_iota(jnp.int32, (1, tq), 1)
    return k_pos <= q_pos


def _mla_prompt_kernel(q_ref, k_ref, vt_ref, o_ref, m_ref, l_ref, acc_ref, p_ref, *, tq, tk, scale):
    i = pl.program_id(1)
    n_heads, _, dk = q_ref.shape
    dv = vt_ref.shape[0]
    q = q_ref[...].reshape(n_heads * tq, dk).astype(MXU_DTYPE)
    _softmax_init(m_ref, l_ref, acc_ref)

    def step(kb, masked):
        start = pl.multiple_of(kb * tk, tk)
        k = k_ref[pl.ds(start, tk), :].astype(MXU_DTYPE)
        v_t = vt_ref[:, pl.ds(start, tk)].astype(MXU_DTYPE)
        raw = _nt_dot(k, q)
        visible = _key_visible(kb, tk, i, tq)

        def logits(h):
            s = raw[:, h * tq:(h + 1) * tq] * scale
            return jnp.where(visible, s, MASK_VALUE) if masked else s

        _softmax_step_cols(logits, v_t, n_heads, tq, m_ref, l_ref, acc_ref, p_ref)

    _causal_chunks(i, tq, tk, step)
    _cols_to_heads(o_ref, acc_ref[...] / l_ref[...], n_heads, tq, dv)


def mla_prompt(q_cat, rows, v_t, n_seq, t, scale, *, tq=128, tk=256):
    n_heads, _, dk = q_cat.shape
    dv = v_t.shape[0]
    tq, tk = _tile(t, tq), _tile(t, tk)
    assert tk % tq == 0
    nq = t // tq
    r = n_heads * tq
    vmem = (2 * (r * dk * 4 + t * dk * 4 + t * dv * 4 + tq * n_heads * dv * 2)
            + r * dk * 2 + 3 * r * dv * 4 + 4 * r * tk * 4 + (6 << 20))
    return pl.pallas_call(
        functools.partial(_mla_prompt_kernel, tq=tq, tk=tk, scale=scale),
        grid=(n_seq, nq),
        in_specs=[pl.BlockSpec((n_heads, tq, dk), lambda b, i: (0, b * nq + i, 0)),
                  pl.BlockSpec((t, dk), lambda b, i: (b, 0)),
                  pl.BlockSpec((dv, t), lambda b, i: (0, b))],
        out_specs=pl.BlockSpec((tq, n_heads * dv), lambda b, i: (b * nq + i, 0)),
        out_shape=jax.ShapeDtypeStruct((n_seq * t, n_heads * dv), MXU_DTYPE),
        scratch_shapes=[pltpu.VMEM((1, r), F32), pltpu.VMEM((1, r), F32), pltpu.VMEM((dv, r), F32),
                        pltpu.VMEM((tk, r), MXU_DTYPE)],
        compiler_params=_params(("parallel", "arbitrary"), vmem),
        name="mla_prompt",
    )(q_cat, rows, v_t)


def _fox_prompt_kernel(q_ref, kv_ref, vt_ref, cum_ref, cumt_ref, o_ref, m_ref, l_ref, acc_ref, p_ref,
                       *, tq, tk, scale, hd):
    i = pl.program_id(1)
    n_heads = q_ref.shape[1] // hd
    q = _heads_to_rows(q_ref[...].astype(MXU_DTYPE), n_heads, hd)
    cum_q = cumt_ref[:, pl.ds(pl.multiple_of(i * tq, tq), tq)]
    _softmax_init(m_ref, l_ref, acc_ref)

    def step(kb, masked):
        start = pl.multiple_of(kb * tk, tk)
        k = kv_ref[pl.ds(start, tk), 0:hd].astype(MXU_DTYPE)
        v_t = vt_ref[:, pl.ds(start, tk)].astype(MXU_DTYPE)
        cum_k = cum_ref[pl.ds(start, tk), :]
        raw = _nt_dot(k, q)
        visible = _key_visible(kb, tk, i, tq)

        def logits(h):
            s = raw[:, h * tq:(h + 1) * tq] * scale + cum_q[h:h + 1, :] - cum_k[:, h:h + 1]
            return jnp.where(visible, s, MASK_VALUE) if masked else s

        _softmax_step_cols(logits, v_t, n_heads, tq, m_ref, l_ref, acc_ref, p_ref)

    _causal_chunks(i, tq, tk, step)
    _cols_to_heads(o_ref, acc_ref[...] / l_ref[...], n_heads, tq, hd)


def fox_prompt(q, kv_rows, v_t, cum, cum_t, n_seq, t, scale, *, tq=128, tk=256):
    width = q.shape[1]
    hd = kv_rows.shape[1] // 2
    n_heads = width // hd
    tq, tk = _tile(t, tq), _tile(t, tk)
    assert tk % tq == 0
    nq = t // tq
    r = n_heads * tq
    vmem = (2 * (2 * tq * width * 2 + t * 2 * hd * 4 + tq * LANES * 4 + n_heads * t * 4)
            + r * hd * 2 + 2 * r * LANES * 4 + 2 * r * hd * 4 + 5 * r * tk * 4 + (6 << 20))
    return pl.pallas_call(
        functools.partial(_fox_prompt_kernel, tq=tq, tk=tk, scale=scale, hd=hd),
        grid=(n_seq, nq),
        in_specs=[pl.BlockSpec((tq, width), lambda b, i: (b * nq + i, 0)),
                  pl.BlockSpec((t, 2 * hd), lambda b, i: (b, 0)),
                  pl.BlockSpec((hd, t), lambda b, i: (0, b)),
                  pl.BlockSpec((t, n_heads), lambda b, i: (b, 0)),
                  pl.BlockSpec((None, n_heads, t), lambda b, i: (b, 0, 0))],
        out_specs=pl.BlockSpec((tq, width), lambda b, i: (b * nq + i, 0)),
        out_shape=jax.ShapeDtypeStruct((n_seq * t, width), MXU_DTYPE),
        scratch_shapes=[pltpu.VMEM((1, r), F32), pltpu.VMEM((1, r), F32), pltpu.VMEM((hd, r), F32),
                        pltpu.VMEM((tk, r), MXU_DTYPE)],
        compiler_params=_params(("parallel", "arbitrary"), vmem),
        name="fox_prompt",
    )(q, kv_rows, v_t, cum, cum_t)


def _cumsum_prompt_kernel(x_ref, o_ref, *, blk):
    t, c = x_ref.shape
    carry = jnp.zeros((1, c), F32)
    for s in range(0, t, blk):
        out = _prefix_sum_rows(x_ref[s:s + blk, :], carry)
        o_ref[s:s + blk, :] = out
        carry = out[blk - 1:blk, :]


def cumsum_prompt(logf, n_seq, t):
    c = logf.shape[1]
    blk = _tile(t, LANES)
    return pl.pallas_call(
        functools.partial(_cumsum_prompt_kernel, blk=blk),
        grid=(n_seq,),
        in_specs=[pl.BlockSpec((t, c), lambda b: (b, 0))],
        out_specs=pl.BlockSpec((t, c), lambda b: (b, 0)),
        out_shape=jax.ShapeDtypeStruct((n_seq * t, c), F32),
        compiler_params=_params(("parallel",), 16 << 20),
    )(logf)


def _dsa_prompt_kernel(qa_ref, qi_ref, wit_ref, rows_ref, vt_ref, bias_ref, o_ref,
                       score_ref, m_ref, l_ref, acc_ref, p_ref, *, tq, tk, topk, scale, w_scale, hd, idx_dim):
    i = pl.program_id(1)
    n_heads = qa_ref.shape[1] // hd
    n_idx = qi_ref.shape[1] // idx_dim
    n_chunks = (i * tq + tq + tk - 1) // tk

    score_ref[...] = jnp.full(score_ref.shape, -jnp.inf, F32)
    qi = _heads_to_rows(qi_ref[...].astype(MXU_DTYPE), n_idx, idx_dim)
    wi = wit_ref[...] * w_scale

    def score_step(kc, _):
        start = pl.multiple_of(kc * tk, tk)
        ki = rows_ref[pl.ds(start, tk), 2 * hd:2 * hd + idx_dim].astype(MXU_DTYPE)
        act = jnp.maximum(_nt_dot(ki, qi), 0.0)
        acc = jnp.zeros((tk, tq), F32)
        for h in range(n_idx):
            acc = acc + wi[h:h + 1, :] * act[:, h * tq:(h + 1) * tq]
        score_ref[pl.ds(start, tk), :] = jnp.where(_key_visible(kc, tk, i, tq), acc, -jnp.inf)
        return 0

    lax.fori_loop(0, n_chunks, score_step, 0)

    score = score_ref[...]
    key = _order_key(score)
    thr = _kth_largest_key(key, topk, axis=0)
    score_ref[...] = jnp.where((key >= thr) & (score > -jnp.inf), 0.0, MASK_VALUE)

    q = _heads_to_rows(qa_ref[...].astype(MXU_DTYPE), n_heads, hd)
    _softmax_init(m_ref, l_ref, acc_ref)
    per_step = tk // LANES

    def attn_step(kb, _):
        start = pl.multiple_of(kb * tk, tk)
        k = rows_ref[pl.ds(start, tk), 0:hd].astype(MXU_DTYPE)
        v_t = vt_ref[:, pl.ds(start, tk)].astype(MXU_DTYPE)
        keep = score_ref[pl.ds(start, tk), :]
        raw = _nt_dot(k, q)
        behind = [jnp.clip(i - (kb * per_step + b), 0, 2) for b in range(per_step)]

        def logits(h):
            bias = jnp.concatenate([bias_ref[off, h] for off in behind], axis=0)
            return raw[:, h * tq:(h + 1) * tq] * scale + bias + keep

        _softmax_step_cols(logits, v_t, n_heads, tq, m_ref, l_ref, acc_ref, p_ref)
        return 0

    lax.fori_loop(0, n_chunks, attn_step, 0)
    _cols_to_heads(o_ref, acc_ref[...] / l_ref[...], n_heads, tq, hd)


def dsa_prompt(q_a, q_i, w_i_t, rows, v_t, bias_t, n_seq, t, scale, w_scale, hd, idx_dim):
    tq = LANES
    tk = 2 * LANES
    assert t % tk == 0
    nq = t // tq
    n_heads = q_a.shape[1] // hd
    r = n_heads * tq
    row_w = rows.shape[1]
    vmem = (2 * (2 * tq * q_a.shape[1] * 4 + tq * q_i.shape[1] * 4 + t * (row_w + hd) * 4 + bias_t.size * 4)
            + 4 * tq * t * 4 + 3 * r * hd * 4 + 3 * tk * q_i.shape[1] // idx_dim * tq * 4 + 3 * tk * r * 4 + (6 << 20))
    return pl.pallas_call(
        functools.partial(_dsa_prompt_kernel, tq=tq, tk=tk, topk=min(DSA_TOPK, t // 4), scale=scale,
                          w_scale=w_scale, hd=hd, idx_dim=idx_dim),
        grid=(n_seq, nq),
        in_specs=[pl.BlockSpec((tq, q_a.shape[1]), lambda b, i: (b * nq + i, 0)),
                  pl.BlockSpec((tq, q_i.shape[1]), lambda b, i: (b * nq + i, 0)),
                  pl.BlockSpec((w_i_t.shape[0], tq), lambda b, i: (0, b * nq + i)),
                  pl.BlockSpec((t, row_w), lambda b, i: (b, 0)),
                  pl.BlockSpec((hd, t), lambda b, i: (0, b)),
                  pl.BlockSpec(bias_t.shape, lambda b, i: (0, 0, 0, 0))],
        out_specs=pl.BlockSpec((tq, q_a.shape[1]), lambda b, i: (b * nq + i, 0)),
        out_shape=jax.ShapeDtypeStruct((n_seq * t, q_a.shape[1]), MXU_DTYPE),
        scratch_shapes=[pltpu.VMEM((t, tq), F32), pltpu.VMEM((1, r), F32), pltpu.VMEM((1, r), F32),
                        pltpu.VMEM((hd, r), F32), pltpu.VMEM((tk, r), MXU_DTYPE)],
        compiler_params=_params(("parallel", "arbitrary"), vmem),
        name="dsa_prompt",
    )(q_a, q_i, w_i_t, rows, v_t, bias_t)


def _page_specs(pool, layer, n_group):
    tile = pool.shape[2:]

    def spec(p):
        return pl.BlockSpec((None, None) + tile, lambda b, g, pt: (layer, pt[b, g * n_group + p], 0, 0))

    return [spec(p) for p in range(n_group)]


def _pad_rows(x, n):
    return jnp.concatenate([x, jnp.zeros((n - x.shape[0], x.shape[1]), x.dtype)], axis=0)


def _lane_pair(ref_a, ref_b, lo, hi):
    return jnp.concatenate([ref_a[lo:hi, :], ref_b[lo:hi, :]], axis=1).astype(MXU_DTYPE)


def _mla_sample_kernel(pt_ref, q_ref, new_ref, *rest, n_group, scale, dv):
    pages = rest[:n_group]
    o_ref, kt_ref, m_ref, l_ref, acc_ref = rest[n_group:]
    g = pl.program_id(1)
    n_heads, tq, dk = q_ref.shape
    rows = n_heads * tq
    width = 2 * pages[0].shape[1]
    q = q_ref[...].reshape(rows, dk).astype(MXU_DTYPE)

    @pl.when(g == 0)
    def _():
        _softmax_init(m_ref, l_ref, acc_ref)

    logits = []
    for pp in range(n_group // 2):
        kt = _lane_pair(pages[2 * pp], pages[2 * pp + 1], 0, dk)
        kt_ref[pp] = kt
        logits.append(_dot(q, kt) * scale)
    s = jnp.concatenate(logits, axis=1)
    m_old = m_ref[...]
    m_new = jnp.maximum(m_old, jnp.max(s, axis=1, keepdims=True))
    alpha = jnp.exp(m_old - m_new)
    prob = jnp.exp(s - m_new)
    l_ref[...] = alpha * l_ref[...] + jnp.sum(prob, axis=1, keepdims=True)
    m_ref[...] = m_new
    prob = prob.astype(MXU_DTYPE)
    pv = _nt_dot(prob[:, 0:width], kt_ref[0, 0:dv, :])
    for pp in range(1, n_group // 2):
        pv = pv + _nt_dot(prob[:, pp * width:(pp + 1) * width], kt_ref[pp, 0:dv, :])
    acc_ref[...] = alpha * acc_ref[...] + pv

    @pl.when(g == pl.num_programs(1) - 1)
    def _():
        k = _pad_rows(new_ref[...], LANES).astype(MXU_DTYPE)
        t_row = lax.broadcasted_iota(jnp.int32, (rows, 1), 0) % tq
        j = lax.broadcasted_iota(jnp.int32, (1, LANES), 1)
        s_new = jnp.where(j <= t_row, _nt_dot(q, k) * scale, MASK_VALUE)
        _softmax_step(s_new, k[:, :dv], m_ref, l_ref, acc_ref)
        _rows_to_heads(o_ref, acc_ref[...] / l_ref[...], n_heads, tq, dv)


def mla_sample(q_cat, rows, pool_t, layer, page_table, dv, row0, tq, scale, *, n_group=32):
    n_heads, _, dk = q_cat.shape
    n_seq, n_pages = page_table.shape
    page = pool_t.shape[3]
    n_group = _tile(n_pages, n_group, 2)
    assert n_group % 2 == 0
    qb0 = row0 // tq
    r = n_heads * tq
    grid_spec = pltpu.PrefetchScalarGridSpec(
        num_scalar_prefetch=1,
        grid=(n_seq, n_pages // n_group),
        in_specs=[pl.BlockSpec((n_heads, tq, dk), lambda b, g, pt: (0, qb0 + b, 0)),
                  pl.BlockSpec((tq, dk), lambda b, g, pt: (qb0 + b, 0))] + _page_specs(pool_t, layer, n_group),
        out_specs=pl.BlockSpec((tq, n_heads * dv), lambda b, g, pt: (b, 0)),
        scratch_shapes=[pltpu.VMEM((n_group // 2, dk, 2 * page), MXU_DTYPE),
                        pltpu.VMEM((r, 1), F32), pltpu.VMEM((r, 1), F32), pltpu.VMEM((r, dv), F32)],
    )
    vmem = n_group * dk * page * (2 * 4 + 2) + 4 * r * n_group * page * 4 + (12 << 20)
    return pl.pallas_call(
        functools.partial(_mla_sample_kernel, n_group=n_group, scale=scale, dv=dv),
        grid_spec=grid_spec,
        out_shape=jax.ShapeDtypeStruct((n_seq * tq, n_heads * dv), F32),
        compiler_params=_params(("parallel", "arbitrary"), vmem),
        name="mla_sample",
    )(page_table, q_cat, rows, *([pool_t] * n_group))


def _split3(x):
    hi = x.astype(MXU_DTYPE)
    r1 = x - hi.astype(F32)
    mid = r1.astype(MXU_DTYPE)
    lo = (r1 - mid.astype(F32)).astype(MXU_DTYPE)
    return hi, mid, lo


def _prefix_sum_lanes(blocks):
    c, n = blocks[0].shape
    tri = (lax.broadcasted_iota(jnp.int32, (n, n), 0) <= lax.broadcasted_iota(jnp.int32, (n, n), 1)).astype(MXU_DTYPE)
    pieces = [piece for x in blocks for piece in _split3(x)]
    out = _dot(jnp.concatenate(pieces, axis=0), tri)
    return [out[3 * i * c:(3 * i + 1) * c] + out[(3 * i + 1) * c:(3 * i + 2) * c]
            + out[(3 * i + 2) * c:(3 * i + 3) * c] for i in range(len(blocks))]


def _fox_sample_kernel(pt_ref, q_ref, new_ref, logf_new_ref, *rest, n_pages, scale, hd, batch):
    kv_pages = rest[:n_pages]
    lf_pages = rest[n_pages:2 * n_pages]
    o_ref, cum_ref, s_ref = rest[2 * n_pages:]
    tq = q_ref.shape[0]
    n_heads = q_ref.shape[1] // hd
    rows = n_heads * tq
    page = kv_pages[0].shape[0]
    past = n_pages * page

    carry = jnp.zeros((n_heads, 1), F32)
    for p0 in range(0, n_pages, batch):
        inner = _prefix_sum_lanes([lf_pages[p][...] for p in range(p0, p0 + batch)])
        for p, cum in zip(range(p0, p0 + batch), inner):
            cum_ref[:, p * page:(p + 1) * page] = cum + carry
            carry = carry + cum[:, page - 1:page]
    cum_new = _prefix_sum_lanes([logf_new_ref[...]])[0] + carry
    cum_ref[:, past:past + page] = cum_new
    pick = (lax.broadcasted_iota(jnp.int32, (tq, page), 0)
            == lax.broadcasted_iota(jnp.int32, (tq, page), 1)).astype(MXU_DTYPE)
    cum_q = sum(_nt_dot(pick, piece) for piece in _split3(cum_new))
    cum_q = jnp.stack([cum_q[:, h:h + 1] for h in range(n_heads)])

    q = _heads_to_rows(q_ref[...].astype(MXU_DTYPE), n_heads, hd)

    def logits(raw, cum_k):
        n = raw.shape[1]
        return (raw.reshape(n_heads, tq, n) * scale + cum_q - cum_k[:, None, :]).reshape(rows, n)

    for p in range(0, n_pages, 2):
        k = jnp.concatenate([kv_pages[p][:, 0:hd], kv_pages[p + 1][:, 0:hd]], axis=0).astype(MXU_DTYPE)
        s_ref[:, p * page:(p + 2) * page] = logits(_nt_dot(q, k), cum_ref[:, p * page:(p + 2) * page])
    kv_new = _pad_rows(new_ref[...], page).astype(MXU_DTYPE)
    t_row = lax.broadcasted_iota(jnp.int32, (rows, 1), 0) % tq
    j = lax.broadcasted_iota(jnp.int32, (1, page), 1)
    s_ref[:, past:past + page] = jnp.where(j <= t_row, logits(_nt_dot(q, kv_new[:, :hd]), cum_new), MASK_VALUE)

    s = s_ref[...]
    prob = jnp.exp(s - jnp.max(s, axis=1, keepdims=True))
    denom = jnp.sum(prob, axis=1, keepdims=True)
    prob = prob.astype(MXU_DTYPE)
    out = _dot(prob[:, past:past + page], kv_new[:, hd:])
    for p in range(0, n_pages, 2):
        v = jnp.concatenate([kv_pages[p][:, hd:], kv_pages[p + 1][:, hd:]], axis=0).astype(MXU_DTYPE)
        out = out + _dot(prob[:, p * page:(p + 2) * page], v)
    _rows_to_heads(o_ref, out / denom, n_heads, tq, hd)


def fox_sample(q, kv_new, logf_new_t, kv_pool, logf_pool_t, layer, page_table, row0, tq, scale):
    width = q.shape[1]
    hd = kv_new.shape[1] // 2
    n_heads = width // hd
    n_seq, n_pages = page_table.shape
    page = kv_pool.shape[2]
    assert n_pages % 2 == 0 and page == logf_pool_t.shape[3]
    qb0 = row0 // tq
    r = n_heads * tq
    n_keys = (n_pages + 1) * page
    grid_spec = pltpu.PrefetchScalarGridSpec(
        num_scalar_prefetch=1,
        grid=(n_seq, 1),
        in_specs=[pl.BlockSpec((tq, width), lambda b, g, pt: (qb0 + b, 0)),
                  pl.BlockSpec((tq, 2 * hd), lambda b, g, pt: (qb0 + b, 0)),
                  pl.BlockSpec((None, n_heads, page), lambda b, g, pt: (b, 0, 0))]
        + _page_specs(kv_pool, layer, n_pages) + _page_specs(logf_pool_t, layer, n_pages),
        out_specs=pl.BlockSpec((tq, width), lambda b, g, pt: (b, 0)),
        scratch_shapes=[pltpu.VMEM((n_heads, n_keys), F32), pltpu.VMEM((r, n_keys), F32)],
    )
    vmem = 2 * n_pages * page * (2 * hd + n_heads) * 4 + 4 * r * n_keys * 4 + (8 << 20)
    return pl.pallas_call(
        functools.partial(_fox_sample_kernel, n_pages=n_pages, scale=scale, hd=hd, batch=_tile(n_pages, 8, 1)),
        grid_spec=grid_spec,
        out_shape=jax.ShapeDtypeStruct((n_seq * tq, width), F32),
        compiler_params=_params(("parallel", "arbitrary"), vmem),
        name="fox_sample",
    )(page_table, q, kv_new, logf_new_t, *([kv_pool] * n_pages), *([logf_pool_t] * n_pages))


def _idx_scores(z, wi, n_idx, tq):
    z = jnp.maximum(z, 0.0)
    acc = jnp.zeros((tq, z.shape[1]), F32)
    for h in range(n_idx):
        acc = acc + wi[:, h:h + 1] * z[h * tq:(h + 1) * tq, :]
    return acc


def _dsa_sample_kernel(pt_ref, qa_ref, qi_ref, wi_ref, new_ref, bias_ref, *rest,
                       n_pages, topk, scale, w_scale, hd, idx_dim):
    pages = rest[:n_pages]
    o_ref, keep_ref, s_ref = rest[n_pages:]
    tq = qa_ref.shape[0]
    n_heads = qa_ref.shape[1] // hd
    n_idx = qi_ref.shape[1] // idx_dim
    rows = n_heads * tq
    page = pages[0].shape[1]
    past = n_pages * page
    new = new_ref[...]
    t_row = lax.broadcasted_iota(jnp.int32, (tq, 1), 0)
    j = lax.broadcasted_iota(jnp.int32, (1, page), 1)

    qi = _heads_to_rows(qi_ref[...].astype(MXU_DTYPE), n_idx, idx_dim)
    wi = wi_ref[...] * w_scale
    for p in range(0, n_pages, 2):
        ki_t = _lane_pair(pages[p], pages[p + 1], 2 * hd, 2 * hd + idx_dim)
        keep_ref[:, p * page:(p + 2) * page] = _idx_scores(_dot(qi, ki_t), wi, n_idx, tq)
    ki_new = _pad_rows(new[:, 2 * hd:2 * hd + idx_dim], page).astype(MXU_DTYPE)
    keep_ref[:, past:past + page] = jnp.where(j <= t_row, _idx_scores(_nt_dot(qi, ki_new), wi, n_idx, tq), -jnp.inf)

    score = keep_ref[...]
    key = _order_key(score)
    thr = _kth_largest_key(key, topk)
    keep_ref[...] = jnp.where((key >= thr) & (score > -jnp.inf), 0.0, MASK_VALUE)

    q = _heads_to_rows(qa_ref[...].astype(MXU_DTYPE), n_heads, hd)

    def logits(raw, bias, keep):
        n = raw.shape[1]
        return (raw.reshape(n_heads, tq, n) * scale + bias + keep[None]).reshape(rows, n)

    far = bias_ref[2]
    for p in range(0, n_pages, 2):
        k_t = _lane_pair(pages[p], pages[p + 1], 0, hd)
        bias = jnp.concatenate([far, bias_ref[1] if p + 2 == n_pages else far], axis=2)
        s_ref[:, p * page:(p + 2) * page] = logits(_dot(q, k_t), bias, keep_ref[:, p * page:(p + 2) * page])
    kv_new = _pad_rows(new[:, 0:2 * hd], page).astype(MXU_DTYPE)
    s_ref[:, past:past + page] = logits(_nt_dot(q, kv_new[:, :hd]), bias_ref[0], keep_ref[:, past:past + page])

    s = s_ref[...]
    prob = jnp.exp(s - jnp.max(s, axis=1, keepdims=True))
    denom = jnp.sum(prob, axis=1, keepdims=True)
    prob = prob.astype(MXU_DTYPE)
    out = _dot(prob[:, past:past + page], kv_new[:, hd:])
    for p in range(0, n_pages, 2):
        out = out + _nt_dot(prob[:, p * page:(p + 2) * page], _lane_pair(pages[p], pages[p + 1], hd, 2 * hd))
    _rows_to_heads(o_ref, out / denom, n_heads, tq, hd)


def dsa_sample(q_a, q_i, w_i, rows, bias, pool_t, layer, page_table, row0, tq, scale, w_scale, hd, idx_dim):
    n_seq, n_pages = page_table.shape
    page = pool_t.shape[3]
    assert page == LANES and n_pages % 2 == 0
    qb0 = row0 // tq
    width = q_a.shape[1]
    n_keys = (n_pages + 1) * page
    r = (width // hd) * tq
    grid_spec = pltpu.PrefetchScalarGridSpec(
        num_scalar_prefetch=1,
        grid=(n_seq, 1),
        in_specs=[pl.BlockSpec((tq, width), lambda b, g, pt: (qb0 + b, 0)),
                  pl.BlockSpec((tq, q_i.shape[1]), lambda b, g, pt: (qb0 + b, 0)),
                  pl.BlockSpec((tq, w_i.shape[1]), lambda b, g, pt: (qb0 + b, 0)),
                  pl.BlockSpec((tq, rows.shape[1]), lambda b, g, pt: (qb0 + b, 0)),
                  pl.BlockSpec(bias.shape, lambda b, g, pt: (0, 0, 0, 0))] + _page_specs(pool_t, layer, n_pages),
        out_specs=pl.BlockSpec((tq, width), lambda b, g, pt: (b, 0)),
        scratch_shapes=[pltpu.VMEM((tq, n_keys), F32), pltpu.VMEM((r, n_keys), F32)],
    )
    vmem = 2 * n_pages * pool_t.shape[2] * page * 4 + 4 * r * n_keys * 4 + (10 << 20)
    return pl.pallas_call(
        functools.partial(_dsa_sample_kernel, n_pages=n_pages, topk=min(DSA_TOPK, (n_pages * page + tq) // 4),
                          scale=scale, w_scale=w_scale, hd=hd, idx_dim=idx_dim),
        grid_spec=grid_spec,
        out_shape=jax.ShapeDtypeStruct((n_seq * tq, width), F32),
        compiler_params=_params(("parallel", "arbitrary"), vmem),
        name="dsa_sample",
    )(page_table, q_a, q_i, w_i, rows, bias, *([pool_t] * n_pages))


def kernel(x_prompt, x_sample, cache_dsa, cache_mla, cache_fox_kv, cache_fox_logf, page_table, p_prompt, p_sample,
           rel_bias, w_in_ab, mla_q_norm, w_uq, mla_kv_norm, w_uk, w_uv, w_o_ab, w_in_c, b_forget, w_o_c,
           ln1_g, ln1_b, ln2_g, ln2_b, w_ffn_up, w_ffn_down, w_router, w_moe_up, w_moe_down,
           w_ple_gate, b_ple_gate, w_ple_proj):
    n_p, t_p, d = x_prompt.shape
    n_s, t_s, _ = x_sample.shape
    depth = ln1_g.shape[0]
    m_p, m_s = n_p * t_p, n_s * t_s
    hd = cache_fox_kv.shape[-1] // 2
    idx_dim = cache_dsa.shape[-1] - 2 * hd
    kv_lora = w_uk.shape[1]
    rope = cache_mla.shape[-1] - kv_lora
    q_lora = w_uq.shape[1]
    nope = w_uk.shape[3]
    h_a = rel_bias.shape[1]
    h_c = b_forget.shape[1]
    ab_cols = w_in_ab.shape[2]
    n_idx_w = (ab_cols - h_a * hd - 2 * hd - idx_dim - q_lora - kv_lora - rope) // (idx_dim + 1)
    past_len = page_table.shape[1] * cache_dsa.shape[2]
    alpha = (2 * depth) ** 0.25
    attn_scale = hd ** -0.5
    mla_scale = (nope + rope) ** -0.5
    idx_w_scale = (n_idx_w * idx_dim) ** -0.5

    c_qa = 0
    c_kv = h_a * hd
    c_qi = c_kv + 2 * hd
    c_tail = c_qi + n_idx_w * idx_dim
    t_wi = idx_dim
    t_cq = t_wi + n_idx_w
    t_ckv = t_cq + q_lora
    t_kr = t_ckv + kv_lora

    x = jnp.concatenate([x_prompt.reshape(m_p, d), x_sample.reshape(m_s, d)], axis=0)
    xb = x.astype(MXU_DTYPE)
    ple = jnp.concatenate([p_prompt.reshape(depth, m_p, -1), p_sample.reshape(depth, m_s, -1)], axis=1)
    pos = jnp.concatenate([jnp.tile(jnp.arange(t_p, dtype=jnp.int32), n_p),
                           past_len + jnp.tile(jnp.arange(t_s, dtype=jnp.int32), n_s)])
    inv = ROPE_THETA ** (-jnp.arange(0, rope, 2, dtype=F32) / rope)
    ang = pos.astype(F32)[:, None] * inv[None, :]
    cos, sin = jnp.cos(ang), jnp.sin(ang)
    bias_p_t = jnp.swapaxes(bias_tables(rel_bias, LANES), 2, 3)
    bias_s = bias_tables(rel_bias, t_s)
    dsa_pool_t = jnp.swapaxes(cache_dsa, 2, 3)
    mla_pool_t = jnp.swapaxes(cache_mla, 2, 3)
    logf_pool_t = jnp.swapaxes(cache_fox_logf, 2, 3)

    new_dsa, new_mla, new_fkv, new_flf = [], [], [], []
    for i in range(depth):
        j = i // 2
        if i % 2 == 0:
            q_a = matmul(xb, w_in_ab, layer=j, out_dtype=F32, col_start=c_qa, n_cols=h_a * hd)
            kv_a = matmul(xb, w_in_ab, layer=j, out_dtype=F32, col_start=c_kv, n_cols=2 * hd)
            q_i = matmul(xb, w_in_ab, layer=j, out_dtype=F32, col_start=c_qi, n_cols=n_idx_w * idx_dim)
            tail = matmul(xb, w_in_ab[j, :, c_tail:], out_dtype=F32)
            w_i = tail[:, t_wi:t_cq]
            cqn, mla_rows = ab_post(tail[:, t_cq:t_ckv], tail[:, t_ckv:t_kr], tail[:, t_kr:], cos, sin,
                                    mla_q_norm[j], mla_kv_norm[j])
            dsa_rows = jnp.concatenate([kv_a, tail[:, :t_wi]], axis=-1)
            q_cat = mla_queries(cqn, w_uq[j], w_uk[j], cos, sin)
            w_uv_heads = jnp.transpose(w_uv[j], (1, 0, 2))

            o_a_p = dsa_prompt(q_a, q_i, jnp.transpose(w_i), dsa_rows, jnp.transpose(kv_a[:, hd:]), bias_p_t,
                               n_p, t_p, attn_scale, idx_w_scale, hd, idx_dim)
            o_a_s = dsa_sample(q_a, q_i, w_i, dsa_rows, bias_s, dsa_pool_t, j, page_table, m_p, t_s,
                               attn_scale, idx_w_scale, hd, idx_dim)
            o_b_p = mla_out(mla_prompt(q_cat, mla_rows, jnp.transpose(mla_rows[:, :kv_lora]), n_p, t_p, mla_scale),
                            w_uv_heads)
            o_b_s = mla_out(mla_sample(q_cat, mla_rows, mla_pool_t, j, page_table, kv_lora, m_p, t_s, mla_scale),
                            w_uv_heads)
            heads = jnp.concatenate([jnp.concatenate([o_a_p, o_a_s.astype(MXU_DTYPE)], axis=0),
                                     jnp.concatenate([o_b_p, o_b_s], axis=0)], axis=1)
            mix = matmul(heads, w_o_ab, layer=j, out_dtype=F32)
            new_dsa.append(dsa_rows)
            new_mla.append(mla_rows)
        else:
            q = matmul(xb, w_in_c, layer=j, out_dtype=F32, col_start=0, n_cols=h_c * hd)
            kv_c = matmul(xb, w_in_c, layer=j, out_dtype=F32, col_start=h_c * hd, n_cols=2 * hd)
            logf = fox_log_forget(xb, w_in_c[j, :, h_c * hd + 2 * hd:], b_forget[j])
            cum_p = cumsum_prompt(logf[:m_p], n_p, t_p)
            cum_t_p = jnp.swapaxes(cum_p.reshape(n_p, t_p, h_c), 1, 2)
            o_p = fox_prompt(q, kv_c, jnp.transpose(kv_c[:, hd:]), cum_p, cum_t_p, n_p, t_p, attn_scale)
            logf_new_t = jnp.pad(jnp.swapaxes(logf[m_p:].reshape(n_s, t_s, h_c), 1, 2),
                                 ((0, 0), (0, 0), (0, logf_pool_t.shape[3] - t_s)))
            o_s = fox_sample(q, kv_c, logf_new_t, cache_fox_kv, logf_pool_t, j, page_table, m_p, t_s, attn_scale)
            heads = jnp.concatenate([o_p, o_s.astype(MXU_DTYPE)], axis=0)
            mix = matmul(heads, w_o_c, layer=j, out_dtype=F32)
            new_fkv.append(kv_c)
            new_flf.append(logf)
        h, hb = deepnorm(x, mix, ln1_g[i], ln1_b[i], alpha)
        if i % 2 == 0:
            ff = matmul(swiglu_up(hb, w_ffn_up, j), w_ffn_down, layer=j, out_dtype=F32)
        else:
            combine = router_combine(hb, w_router[j])
            ff = moe_down(moe_up(hb, w_moe_up, j), w_moe_down, j, jnp.transpose(combine)[:, :, None])
        h, hb = deepnorm(h, ff, ln2_g[i], ln2_b[i], alpha)
        x, xb = ple_mix(h, hb, w_ple_gate, b_ple_gate[i], ple, w_ple_proj, i)

    def split(rows_list):
        a = jnp.stack(rows_list)
        return a[:, :m_p].reshape(len(rows_list), n_p, t_p, -1), a[:, m_p:].reshape(len(rows_list), n_s, t_s, -1)

    dsa_p, dsa_s = split(new_dsa)
    mla_p, mla_s = split(new_mla)
    fkv_p, fkv_s = split(new_fkv)
    flf_p, flf_s = split(new_flf)
    return (x[:m_p].reshape(n_p, t_p, d), x[m_p:].reshape(n_s, t_s, d),
            dsa_p, mla_p, fkv_p, flf_p, dsa_s, mla_s, fkv_s, flf_s)
```

```python
import functools
import math

import numpy as np
import jax
import jax.numpy as jnp
from jax import lax
from jax.experimental import pallas as pl
from jax.experimental.pallas import tpu as pltpu

DSA_TOPK = 256
N_BUCKETS = 32
MAX_DISTANCE = 128
ROPE_THETA = 10000.0
LN_EPS = 1e-5
RMS_EPS = 1e-6
MOE_TOP_K = 2

MXU_DTYPE = jnp.bfloat16
MASK_VALUE = -1e30
V7X_VMEM_BYTES = 64 << 20
VMEM_CAP_BYTES = V7X_VMEM_BYTES - (6 << 20)
LANES = 128
F32 = jnp.float32


def _tile(n, target, align=8):
    if n <= target:
        return n
    for d in range(target, 0, -1):
        if n % d == 0 and d % align == 0:
            return d
    return n


def _params(semantics, vmem_bytes):
    limit = int(min(max(vmem_bytes, 16 << 20), VMEM_CAP_BYTES))
    return pltpu.CompilerParams(dimension_semantics=semantics, vmem_limit_bytes=limit)


def _nt_dot(a, b):
    return lax.dot_general(a, b, (((1,), (1,)), ((), ())), preferred_element_type=F32)


def _dot(a, b):
    return jnp.dot(a, b, preferred_element_type=F32)


def _mm_kernel(x_ref, w_ref, o_ref, *scratch, nk, transposed):
    contract = _nt_dot if transposed else _dot
    part = contract(x_ref[...].astype(MXU_DTYPE), w_ref[...].astype(MXU_DTYPE))
    if nk == 1:
        o_ref[...] = part.astype(o_ref.dtype)
        return
    acc_ref, = scratch
    k = pl.program_id(2)

    @pl.when(k == 0)
    def _():
        acc_ref[...] = part

    @pl.when(k > 0)
    def _():
        acc_ref[...] += part

    @pl.when(k == nk - 1)
    def _():
        o_ref[...] = acc_ref[...].astype(o_ref.dtype)


def matmul(x, w, *, out_dtype, layer=None, transposed=False, col_start=0, n_cols=None, tm=1024, tn=256, tk=8192):
    m, kdim = x.shape
    n_total = w.shape[-2] if transposed else w.shape[-1]
    n_cols = n_total - col_start if n_cols is None else n_cols
    tm = _tile(m, tm)
    tk = _tile(kdim, tk, LANES)
    if n_cols % LANES == 0 and col_start % LANES == 0:
        tn = _tile(n_cols, tn, LANES)
        while col_start % tn:
            tn -= LANES
    else:
        assert col_start == 0 and n_cols == n_total
        tn = n_cols
    off = col_start // tn
    xb, ob = x.dtype.itemsize, jnp.dtype(out_dtype).itemsize
    cast_x = 0 if x.dtype == MXU_DTYPE else 2

    tn_pad = -(-tn // LANES) * LANES

    def estimate(tk):
        return (2 * (tm * tk * xb + tk * tn_pad * 4 + tm * tn_pad * ob) + 2 * tm * tn_pad * 4 + tk * tn_pad * 2
                + tm * tk * cast_x + (4 << 20))

    while estimate(tk) > VMEM_CAP_BYTES * 7 // 8 and tk % (2 * LANES) == 0:
        tk //= 2
    nk = kdim // tk
    vmem = estimate(tk)
    w_block = (tn, tk) if transposed else (tk, tn)

    def w_index(i, j, k):
        pos = (j + off, k) if transposed else (k, j + off)
        return pos if layer is None else (layer,) + pos

    w_spec = pl.BlockSpec(w_block if layer is None else (None,) + w_block, w_index)
    return pl.pallas_call(
        functools.partial(_mm_kernel, nk=nk, transposed=transposed),
        grid=(m // tm, n_cols // tn, nk),
        in_specs=[pl.BlockSpec((tm, tk), lambda i, j, k: (i, k)), w_spec],
        out_specs=pl.BlockSpec((tm, tn), lambda i, j, k: (i, j)),
        out_shape=jax.ShapeDtypeStruct((m, n_cols), out_dtype),
        scratch_shapes=[pltpu.VMEM((tm, tn), F32)] if nk > 1 else [],
        compiler_params=_params(("parallel", "parallel", "arbitrary"), vmem),
        name="matmul",
    )(x, w)


def _swiglu_up_kernel(x_ref, wg_ref, wu_ref, o_ref):
    x = x_ref[...]
    gate = _dot(x, wg_ref[...].astype(MXU_DTYPE))
    up = _dot(x, wu_ref[...].astype(MXU_DTYPE))
    o_ref[...] = (gate * jax.nn.sigmoid(gate) * up).astype(o_ref.dtype)


def swiglu_up(xb, w_up, layer, *, tm=1024, tn=256):
    m, d = xb.shape
    f = w_up.shape[2] // 2
    tm, tn = _tile(m, tm), _tile(f, tn, LANES)
    nf = f // tn
    vmem = 2 * (tm * d * 2 + 2 * d * tn * 4 + tm * tn * 2) + 2 * d * tn * 2 + 3 * tm * tn * 4 + (4 << 20)
    return pl.pallas_call(
        _swiglu_up_kernel,
        grid=(m // tm, nf),
        in_specs=[pl.BlockSpec((tm, d), lambda i, j: (i, 0)),
                  pl.BlockSpec((None, d, tn), lambda i, j: (layer, 0, j)),
                  pl.BlockSpec((None, d, tn), lambda i, j: (layer, 0, j + nf))],
        out_specs=pl.BlockSpec((tm, tn), lambda i, j: (i, j)),
        out_shape=jax.ShapeDtypeStruct((m, f), MXU_DTYPE),
        compiler_params=_params(("parallel", "parallel"), vmem),
        name="swiglu_up",
    )(xb, w_up, w_up)


def moe_up(xb, w_up, layer, *, tm=1024, tn=256):
    m, d = xb.shape
    _, n_exp, _, f2 = w_up.shape
    f = f2 // 2
    tm, tn = _tile(m, tm), _tile(f, tn, LANES)
    nf = f // tn
    vmem = 2 * (tm * d * 2 + 2 * d * tn * 4 + tm * tn * 2) + 2 * d * tn * 2 + 3 * tm * tn * 4 + (4 << 20)
    return pl.pallas_call(
        _swiglu_up_kernel,
        grid=(m // tm, n_exp, nf),
        in_specs=[pl.BlockSpec((tm, d), lambda i, e, j: (i, 0)),
                  pl.BlockSpec((None, None, d, tn), lambda i, e, j: (layer, e, 0, j)),
                  pl.BlockSpec((None, None, d, tn), lambda i, e, j: (layer, e, 0, j + nf))],
        out_specs=pl.BlockSpec((tm, tn), lambda i, e, j: (i, e * nf + j)),
        out_shape=jax.ShapeDtypeStruct((m, n_exp * f), MXU_DTYPE),
        compiler_params=_params(("parallel", "parallel", "parallel"), vmem),
        name="moe_up",
    )(xb, w_up, w_up)


def _moe_down_kernel(a_ref, w_ref, c_ref, o_ref, acc_ref, *, n_exp):
    e = pl.program_id(2)
    part = c_ref[...] * _dot(a_ref[...], w_ref[...].astype(MXU_DTYPE))

    @pl.when(e == 0)
    def _():
        acc_ref[...] = part

    @pl.when(e > 0)
    def _():
        acc_ref[...] += part

    @pl.when(e == n_exp - 1)
    def _():
        o_ref[...] = acc_ref[...]


def moe_down(act, w_down, layer, combine_t, *, tm=1024, tn=1024):
    m = act.shape[0]
    _, n_exp, f, d = w_down.shape
    tm, tn = _tile(m, tm), _tile(d, tn, LANES)
    vmem = 2 * (tm * f * 2 + f * tn * 4 + tm * LANES * 4 + tm * tn * 4) + 3 * tm * tn * 4 + f * tn * 2 + (4 << 20)
    return pl.pallas_call(
        functools.partial(_moe_down_kernel, n_exp=n_exp),
        grid=(m // tm, d // tn, n_exp),
        in_specs=[pl.BlockSpec((tm, f), lambda i, j, e: (i, e)),
                  pl.BlockSpec((None, None, f, tn), lambda i, j, e: (layer, e, 0, j)),
                  pl.BlockSpec((None, tm, 1), lambda i, j, e: (e, i, 0))],
        out_specs=pl.BlockSpec((tm, tn), lambda i, j, e: (i, j)),
        out_shape=jax.ShapeDtypeStruct((m, d), F32),
        scratch_shapes=[pltpu.VMEM((tm, tn), F32)],
        compiler_params=_params(("parallel", "parallel", "arbitrary"), vmem),
        name="moe_down",
    )(act, w_down, combine_t)


def _router_kernel(x_ref, w_ref, o_ref):
    logits = _dot(x_ref[...], w_ref[...].astype(MXU_DTYPE))
    n_exp = logits.shape[1]
    lane = lax.broadcasted_iota(jnp.int32, logits.shape, 1)
    m1 = jnp.max(logits, axis=1, keepdims=True)
    i1 = jnp.min(jnp.where(logits == m1, lane, n_exp), axis=1, keepdims=True)
    rest = jnp.where(lane == i1, -jnp.inf, logits)
    m2 = jnp.max(rest, axis=1, keepdims=True)
    i2 = jnp.min(jnp.where(rest == m2, lane, n_exp), axis=1, keepdims=True)
    e2 = jnp.exp(m2 - m1)
    denom = 1.0 + e2
    o_ref[...] = jnp.where(lane == i1, 1.0 / denom, 0.0) + jnp.where(lane == i2, e2 / denom, 0.0)


def router_combine(xb, w_router, *, tm=1024):
    m, d = xb.shape
    n_exp = w_router.shape[1]
    tm = _tile(m, tm)
    vmem = 2 * (tm * d * 2 + d * LANES * 4 + tm * LANES * 4) + (8 << 20)
    return pl.pallas_call(
        _router_kernel,
        grid=(m // tm,),
        in_specs=[pl.BlockSpec((tm, d), lambda i: (i, 0)),
                  pl.BlockSpec((d, n_exp), lambda i: (0, 0))],
        out_specs=pl.BlockSpec((tm, n_exp), lambda i: (i, 0)),
        out_shape=jax.ShapeDtypeStruct((m, n_exp), F32),
        compiler_params=_params(("parallel",), vmem),
    )(xb, w_router)


def _deepnorm_kernel(x_ref, y_ref, g_ref, b_ref, o_ref, ob_ref, *, alpha):
    z = alpha * x_ref[...] + y_ref[...]
    mu = jnp.mean(z, axis=-1, keepdims=True)
    zc = z - mu
    var = jnp.mean(zc * zc, axis=-1, keepdims=True)
    out = zc * lax.rsqrt(var + LN_EPS) * g_ref[...] + b_ref[...]
    o_ref[...] = out
    ob_ref[...] = out.astype(ob_ref.dtype)


def deepnorm(x, y, g, b, alpha, *, tm=256):
    m, d = x.shape
    tm = _tile(m, tm)
    row = pl.BlockSpec((tm, d), lambda i: (i, 0))
    vec = pl.BlockSpec((1, d), lambda i: (0, 0))
    return pl.pallas_call(
        functools.partial(_deepnorm_kernel, alpha=alpha),
        grid=(m // tm,),
        in_specs=[row, row, vec, vec],
        out_specs=[row, row],
        out_shape=[jax.ShapeDtypeStruct((m, d), F32), jax.ShapeDtypeStruct((m, d), MXU_DTYPE)],
        compiler_params=_params(("parallel",), 12 * tm * d * 4 + (4 << 20)),
    )(x, y, g.reshape(1, d), b.reshape(1, d))


def _ple_kernel(hb_ref, wg_ref, bg_ref, p_ref, wp_ref, h_ref, o_ref, ob_ref):
    gate = jax.nn.sigmoid(_dot(hb_ref[...], wg_ref[...].astype(MXU_DTYPE)) + bg_ref[...])
    emb = _dot(p_ref[...].astype(MXU_DTYPE), wp_ref[...].astype(MXU_DTYPE))
    out = h_ref[...] + gate * emb
    o_ref[...] = out
    ob_ref[...] = out.astype(ob_ref.dtype)


def ple_mix(h, hb, w_gate, b_gate, p, w_proj, layer, *, tm=1024, tn=256):
    m, d = h.shape
    pd = p.shape[2]
    tm, tn = _tile(m, tm), _tile(d, tn, LANES)
    vmem = 2 * (tm * d * 2 + d * tn * 4 + tm * pd * 4 + pd * tn * 4 + 3 * tm * tn * 4) + d * tn * 2 + 4 * tm * tn * 4 + (4 << 20)
    tile = pl.BlockSpec((tm, tn), lambda i, j: (i, j))
    return pl.pallas_call(
        _ple_kernel,
        grid=(m // tm, d // tn),
        in_specs=[pl.BlockSpec((tm, d), lambda i, j: (i, 0)),
                  pl.BlockSpec((None, d, tn), lambda i, j: (layer, 0, j)),
                  pl.BlockSpec((1, tn), lambda i, j: (0, j)),
                  pl.BlockSpec((None, tm, pd), lambda i, j: (layer, i, 0)),
                  pl.BlockSpec((None, pd, tn), lambda i, j: (layer, 0, j)),
                  tile],
        out_specs=[tile, tile],
        out_shape=[jax.ShapeDtypeStruct((m, d), F32), jax.ShapeDtypeStruct((m, d), MXU_DTYPE)],
        compiler_params=_params(("parallel", "parallel"), vmem),
        name="ple_mix",
    )(hb, w_gate, b_gate.reshape(1, d), p, w_proj, h)


def _rms(x, g):
    return x * lax.rsqrt(jnp.mean(x * x, axis=-1, keepdims=True) + RMS_EPS) * g


def _ab_post_kernel(cq_ref, ckv_ref, k1_ref, k2_ref, cos_ref, sin_ref, gq_ref, gkv_ref,
                    cqn_ref, ckvn_ref, r1_ref, r2_ref):
    cqn_ref[...] = _rms(cq_ref[...], gq_ref[...]).astype(cqn_ref.dtype)
    ckvn_ref[...] = _rms(ckv_ref[...], gkv_ref[...])
    x1, x2, cos, sin = k1_ref[...], k2_ref[...], cos_ref[...], sin_ref[...]
    r1_ref[...] = x1 * cos - x2 * sin
    r2_ref[...] = x1 * sin + x2 * cos


def ab_post(c_q, c_kv, k_r, cos, sin, q_norm, kv_norm, *, tm=512):
    m, ql = c_q.shape
    kl, half = c_kv.shape[1], cos.shape[1]
    tm = _tile(m, tm)

    def row(n):
        return pl.BlockSpec((tm, n), lambda i: (i, 0))

    def vec(n):
        return pl.BlockSpec((1, n), lambda i: (0, 0))

    cqn, ckvn, r1, r2 = pl.pallas_call(
        _ab_post_kernel,
        grid=(m // tm,),
        in_specs=[row(ql), row(kl), row(half), row(half), row(half), row(half), vec(ql), vec(kl)],
        out_specs=[row(ql), row(kl), row(half), row(half)],
        out_shape=[jax.ShapeDtypeStruct((m, ql), MXU_DTYPE), jax.ShapeDtypeStruct((m, kl), F32),
                   jax.ShapeDtypeStruct((m, half), F32), jax.ShapeDtypeStruct((m, half), F32)],
        compiler_params=_params(("parallel",), 32 << 20),
    )(c_q, c_kv, k_r[:, :half], k_r[:, half:], cos, sin, q_norm.reshape(1, ql), kv_norm.reshape(1, kl))
    return cqn, jnp.concatenate([ckvn, r1, r2], axis=-1)


def _mla_q_kernel(c_ref, wn_ref, w1_ref, w2_ref, wk_ref, cos_ref, sin_ref, o_ref, *, kl, half):
    c = c_ref[...]
    q_nope = _dot(c, wn_ref[...].astype(MXU_DTYPE))
    x1 = _dot(c, w1_ref[...].astype(MXU_DTYPE))
    x2 = _dot(c, w2_ref[...].astype(MXU_DTYPE))
    cos, sin = cos_ref[...], sin_ref[...]
    o_ref[:, 0:kl] = _nt_dot(q_nope.astype(MXU_DTYPE), wk_ref[...].astype(MXU_DTYPE))
    o_ref[:, kl:kl + half] = x1 * cos - x2 * sin
    o_ref[:, kl + half:kl + 2 * half] = x1 * sin + x2 * cos


def mla_queries(cqn, w_uq, w_uk, cos, sin, *, tm=1024):
    m, ql = cqn.shape
    kl, n_heads, nope = w_uk.shape
    half = cos.shape[1]
    tm = _tile(m, tm)
    w_heads = jnp.transpose(w_uq, (1, 0, 2))
    wk_heads = jnp.transpose(w_uk, (1, 0, 2))
    return pl.pallas_call(
        functools.partial(_mla_q_kernel, kl=kl, half=half),
        grid=(m // tm, n_heads),
        in_specs=[pl.BlockSpec((tm, ql), lambda i, h: (i, 0)),
                  pl.BlockSpec((None, ql, nope), lambda i, h: (h, 0, 0)),
                  pl.BlockSpec((None, ql, half), lambda i, h: (h, 0, 0)),
                  pl.BlockSpec((None, ql, half), lambda i, h: (h, 0, 0)),
                  pl.BlockSpec((None, kl, nope), lambda i, h: (h, 0, 0)),
                  pl.BlockSpec((tm, half), lambda i, h: (i, 0)),
                  pl.BlockSpec((tm, half), lambda i, h: (i, 0))],
        out_specs=pl.BlockSpec((None, tm, kl + 2 * half), lambda i, h: (h, i, 0)),
        out_shape=jax.ShapeDtypeStruct((n_heads, m, kl + 2 * half), F32),
        compiler_params=_params(("parallel", "parallel"), 32 << 20),
    )(cqn, w_heads[:, :, :nope], w_heads[:, :, nope:nope + half], w_heads[:, :, nope + half:], wk_heads, cos, sin)


def _fox_gate_kernel(x_ref, w_ref, b_ref, o_ref):
    z = _dot(x_ref[...], w_ref[...].astype(MXU_DTYPE)) + b_ref[...]
    o_ref[...] = jnp.minimum(z, 0.0) - jnp.log1p(jnp.exp(-jnp.abs(z)))


def fox_log_forget(xb, w_f, b_f, *, tm=1024):
    m, d = xb.shape
    h = w_f.shape[1]
    tm = _tile(m, tm)
    return pl.pallas_call(
        _fox_gate_kernel,
        grid=(m // tm,),
        in_specs=[pl.BlockSpec((tm, d), lambda i: (i, 0)),
                  pl.BlockSpec((d, h), lambda i: (0, 0)),
                  pl.BlockSpec((1, h), lambda i: (0, 0))],
        out_specs=pl.BlockSpec((tm, h), lambda i: (i, 0)),
        out_shape=jax.ShapeDtypeStruct((m, h), F32),
        compiler_params=_params(("parallel",), 2 * (tm * d * 2 + d * LANES * 4 + tm * LANES * 4) + (8 << 20)),
    )(xb, w_f, b_f.reshape(1, h))


def _softmax_init(m_ref, l_ref, acc_ref):
    m_ref[...] = jnp.full(m_ref.shape, MASK_VALUE, F32)
    l_ref[...] = jnp.zeros(l_ref.shape, F32)
    acc_ref[...] = jnp.zeros(acc_ref.shape, F32)


def _softmax_step(s, v, m_ref, l_ref, acc_ref):
    m_old = m_ref[...]
    m_new = jnp.maximum(m_old, jnp.max(s, axis=1, keepdims=True))
    alpha = jnp.exp(m_old - m_new)
    p = jnp.exp(s - m_new)
    l_ref[...] = alpha * l_ref[...] + jnp.sum(p, axis=1, keepdims=True)
    acc_ref[...] = alpha * acc_ref[...] + _dot(p.astype(MXU_DTYPE), v)
    m_ref[...] = m_new


def _softmax_step_cols(logits_of_head, v_t, n_heads, tq, m_ref, l_ref, acc_ref, p_ref):
    for h in range(n_heads):
        c = slice(h * tq, (h + 1) * tq)
        s = logits_of_head(h)
        m_old = m_ref[:, c]
        m_new = jnp.maximum(m_old, jnp.max(s, axis=0, keepdims=True))
        alpha = jnp.exp(m_old - m_new)
        p = jnp.exp(s - m_new)
        l_ref[:, c] = alpha * l_ref[:, c] + jnp.sum(p, axis=0, keepdims=True)
        m_ref[:, c] = m_new
        acc_ref[:, c] = alpha * acc_ref[:, c]
        p_ref[:, c] = p.astype(p_ref.dtype)
    acc_ref[...] += _dot(v_t, p_ref[...])


def _cols_to_heads(o_ref, o_t, n_heads, tq, dv):
    for h in range(n_heads):
        o_ref[:, h * dv:(h + 1) * dv] = o_t[:, h * tq:(h + 1) * tq].T.astype(o_ref.dtype)


def _heads_to_rows(q, n_heads, hd):
    return jnp.concatenate([q[:, h * hd:(h + 1) * hd] for h in range(n_heads)], axis=0)


def _rows_to_heads(o_ref, o, n_heads, t, hd):
    for h in range(n_heads):
        o_ref[:, h * hd:(h + 1) * hd] = o[h * t:(h + 1) * t, :].astype(o_ref.dtype)


def _prefix_sum_rows(x, carry):
    n = x.shape[0]
    tri = (lax.broadcasted_iota(jnp.int32, (n, n), 1) <= lax.broadcasted_iota(jnp.int32, (n, n), 0)).astype(MXU_DTYPE)
    hi = x.astype(MXU_DTYPE)
    r1 = x - hi.astype(F32)
    mid = r1.astype(MXU_DTYPE)
    lo = (r1 - mid.astype(F32)).astype(MXU_DTYPE)
    return _dot(tri, hi) + _dot(tri, mid) + _dot(tri, lo) + carry


def _order_key(score):
    bits = lax.bitcast_convert_type(score, jnp.int32)
    return jnp.where(bits < 0, bits ^ jnp.int32(0x7FFFFFFF), bits)


def _count(mask, axis):
    step = 8 if axis == 0 else LANES
    hit = jnp.where(mask, 1.0, 0.0)
    parts = [lax.slice_in_dim(hit, s, s + step, axis=axis) for s in range(0, mask.shape[axis], step)]
    while len(parts) > 1:
        parts = [a + b for a, b in zip(parts[0::2], parts[1::2])] + parts[len(parts) - len(parts) % 2:]
    return jnp.sum(parts[0], axis=axis, keepdims=True)


def _kth_largest_key(key, k, axis):
    shape = tuple(1 if a == axis else n for a, n in enumerate(key.shape))
    int_min = jnp.int32(-2 ** 31)
    thr = jnp.where(_count(key >= jnp.zeros(shape, jnp.int32), axis) >= k, jnp.int32(0), int_min)

    def single(thr, bit):
        cand = thr | bit
        return jnp.where(_count(key >= cand, axis) >= k, cand, thr)

    def body(it, thr):
        hi = lax.shift_left(jnp.int32(1), jnp.int32(30) - 2 * it)
        lo = lax.shift_right_logical(hi, jnp.int32(1))
        c_hi, c_lo, c_both = thr | hi, thr | lo, thr | hi | lo
        n_hi, n_lo, n_both = _count(key >= c_hi, axis), _count(key >= c_lo, axis), _count(key >= c_both, axis)
        return jnp.where(n_both >= k, c_both, jnp.where(n_hi >= k, c_hi, jnp.where(n_lo >= k, c_lo, thr)))

    if key.size <= 128 * 8 * LANES:
        thr = lax.fori_loop(0, 15, body, thr)
        return single(thr, jnp.int32(1))
    return lax.fori_loop(0, 31, lambda it, t: single(t, lax.shift_left(jnp.int32(1), jnp.int32(30) - it)), thr)


def _running_count(mask, axis):
    n = mask.shape[axis]
    ones = jnp.where(mask, 1.0, 0.0)
    if axis == 0:
        blk = _tile(n, 2 * LANES, LANES)
        tri = (lax.broadcasted_iota(jnp.int32, (blk, blk), 1)
               <= lax.broadcasted_iota(jnp.int32, (blk, blk), 0)).astype(MXU_DTYPE)
        carry = jnp.zeros((1, mask.shape[1]), F32)
        out = []
        for s in range(0, n, blk):
            part = ones[s:s + blk, :]
            incl = _dot(tri, part.astype(MXU_DTYPE))
            out.append(incl - part + carry)
            carry = carry + incl[blk - 1:blk, :]
        return jnp.concatenate(out, axis=0)
    r = mask.shape[0]
    tri = (lax.broadcasted_iota(jnp.int32, (LANES, LANES), 0)
           <= lax.broadcasted_iota(jnp.int32, (LANES, LANES), 1)).astype(MXU_DTYPE)
    parts = [ones[:, s:s + LANES] for s in range(0, n, LANES)]
    incl = _dot(jnp.concatenate(parts, axis=0).astype(MXU_DTYPE), tri)
    carry = jnp.zeros((r, 1), F32)
    out = []
    for b, part in enumerate(parts):
        inc = incl[b * r:(b + 1) * r, :]
        out.append(inc - part + carry)
        carry = carry + inc[:, LANES - 1:LANES]
    return jnp.concatenate(out, axis=1)


def _topk_keep_mask(score, k, axis):
    key = _order_key(score)
    thr = _kth_largest_key(key, k, axis)
    above = key > thr
    tied = key == thr
    need = k - _count(above, axis)
    keep = (above | (tied & (_running_count(tied, axis) < need))) & (score > -jnp.inf)
    return jnp.where(keep, 0.0, MASK_VALUE)


def bias_tables(rel_bias, tq):
    max_exact = N_BUCKETS // 2
    d = np.arange(0, 2 * LANES + tq)
    ratio = np.log(np.maximum(d, 1).astype(np.float32) / max_exact) / math.log(MAX_DISTANCE / max_exact)
    large = np.minimum(max_exact + (ratio * (N_BUCKETS - max_exact)).astype(np.int32), N_BUCKETS - 1)
    bucket = np.where(d < max_exact, d, large)
    i = np.arange(tq)[:, None]
    j = np.arange(LANES)[None, :]
    dist = np.stack([np.maximum(i - j, 0), i - j + LANES, np.full((tq, LANES), 2 * LANES)])
    one_hot = (bucket[dist][..., None] == np.arange(N_BUCKETS)).astype(np.float32)
    return jnp.einsum("abcn,nh->ahbc", one_hot, rel_bias.astype(F32), precision=lax.Precision.HIGHEST)


def _mla_out_kernel(o_ref, w_ref, out_ref):
    out_ref[...] = _dot(o_ref[...].astype(MXU_DTYPE), w_ref[...].astype(MXU_DTYPE)).astype(out_ref.dtype)


def mla_out(o_lat, w_uv_heads, *, tm=1024):
    m = o_lat.shape[0]
    n_heads, dv, hd = w_uv_heads.shape
    tm = _tile(m, tm)
    return pl.pallas_call(
        _mla_out_kernel,
        grid=(m // tm, n_heads),
        in_specs=[pl.BlockSpec((tm, dv), lambda i, h: (i, h)),
                  pl.BlockSpec((None, dv, hd), lambda i, h: (h, 0, 0))],
        out_specs=pl.BlockSpec((tm, hd), lambda i, h: (i, h)),
        out_shape=jax.ShapeDtypeStruct((m, n_heads * hd), MXU_DTYPE),
        compiler_params=_params(("parallel", "parallel"), 24 << 20),
        name="mla_out",
    )(o_lat, w_uv_heads)


def _causal_chunks(i, tq, tk, step):
    n_full = (i * tq + 1) // tk

    def body(kb, carry):
        step(kb, False)
        return carry

    lax.fori_loop(0, n_full, body, 0)
    step(n_full, True)


def _key_visible(kb, tk, i, tq):
    k_pos = kb * tk + lax.broadcasted_iota(jnp.int32, (tk, 1), 0)
    q_pos = i * tq + lax.broadcasted_iota(jnp.int32, (1, tq), 1)
    return k_pos <= q_pos


def _mla_prompt_kernel(q_ref, k_ref, vt_ref, o_ref, m_ref, l_ref, acc_ref, p_ref, *, tq, tk, scale):
    i = pl.program_id(1)
    n_heads, _, dk = q_ref.shape
    dv = vt_ref.shape[0]
    q = q_ref[...].reshape(n_heads * tq, dk).astype(MXU_DTYPE)
    _softmax_init(m_ref, l_ref, acc_ref)

    def step(kb, masked):
        start = pl.multiple_of(kb * tk, tk)
        k = k_ref[pl.ds(start, tk), :].astype(MXU_DTYPE)
        v_t = vt_ref[:, pl.ds(start, tk)].astype(MXU_DTYPE)
        raw = _nt_dot(k, q)
        visible = _key_visible(kb, tk, i, tq)

        def logits(h):
            s = raw[:, h * tq:(h + 1) * tq] * scale
            return jnp.where(visible, s, MASK_VALUE) if masked else s

        _softmax_step_cols(logits, v_t, n_heads, tq, m_ref, l_ref, acc_ref, p_ref)

    _causal_chunks(i, tq, tk, step)
    _cols_to_heads(o_ref, acc_ref[...] / l_ref[...], n_heads, tq, dv)


def mla_prompt(q_cat, rows, v_t, n_seq, t, scale, *, tq=128, tk=256):
    n_heads, _, dk = q_cat.shape
    dv = v_t.shape[0]
    tq, tk = _tile(t, tq), _tile(t, tk)
    assert tk % tq == 0
    nq = t // tq
    r = n_heads * tq
    vmem = (2 * (r * dk * 4 + t * dk * 4 + t * dv * 4 + tq * n_heads * dv * 2)
            + r * dk * 2 + 3 * r * dv * 4 + 4 * r * tk * 4 + (6 << 20))
    return pl.pallas_call(
        functools.partial(_mla_prompt_kernel, tq=tq, tk=tk, scale=scale),
        grid=(n_seq, nq),
        in_specs=[pl.BlockSpec((n_heads, tq, dk), lambda b, i: (0, b * nq + i, 0)),
                  pl.BlockSpec((t, dk), lambda b, i: (b, 0)),
                  pl.BlockSpec((dv, t), lambda b, i: (0, b))],
        out_specs=pl.BlockSpec((tq, n_heads * dv), lambda b, i: (b * nq + i, 0)),
        out_shape=jax.ShapeDtypeStruct((n_seq * t, n_heads * dv), MXU_DTYPE),
        scratch_shapes=[pltpu.VMEM((1, r), F32), pltpu.VMEM((1, r), F32), pltpu.VMEM((dv, r), F32),
                        pltpu.VMEM((tk, r), MXU_DTYPE)],
        compiler_params=_params(("parallel", "arbitrary"), vmem),
        name="mla_prompt",
    )(q_cat, rows, v_t)


def _fox_prompt_kernel(q_ref, kv_ref, vt_ref, cum_ref, cumt_ref, o_ref, m_ref, l_ref, acc_ref, p_ref,
                       *, tq, tk, scale, hd):
    i = pl.program_id(1)
    n_heads = q_ref.shape[1] // hd
    q = _heads_to_rows(q_ref[...].astype(MXU_DTYPE), n_heads, hd)
    cum_q = cumt_ref[:, pl.ds(pl.multiple_of(i * tq, tq), tq)]
    _softmax_init(m_ref, l_ref, acc_ref)

    def step(kb, masked):
        start = pl.multiple_of(kb * tk, tk)
        k = kv_ref[pl.ds(start, tk), 0:hd].astype(MXU_DTYPE)
        v_t = vt_ref[:, pl.ds(start, tk)].astype(MXU_DTYPE)
        cum_k = cum_ref[pl.ds(start, tk), :]
        raw = _nt_dot(k, q)
        visible = _key_visible(kb, tk, i, tq)

        def logits(h):
            s = raw[:, h * tq:(h + 1) * tq] * scale + cum_q[h:h + 1, :] - cum_k[:, h:h + 1]
            return jnp.where(visible, s, MASK_VALUE) if masked else s

        _softmax_step_cols(logits, v_t, n_heads, tq, m_ref, l_ref, acc_ref, p_ref)

    _causal_chunks(i, tq, tk, step)
    _cols_to_heads(o_ref, acc_ref[...] / l_ref[...], n_heads, tq, hd)


def fox_prompt(q, kv_rows, v_t, cum, cum_t, n_seq, t, scale, *, tq=128, tk=256):
    width = q.shape[1]
    hd = kv_rows.shape[1] // 2
    n_heads = width // hd
    tq, tk = _tile(t, tq), _tile(t, tk)
    assert tk % tq == 0
    nq = t // tq
    r = n_heads * tq
    vmem = (2 * (2 * tq * width * 2 + t * 2 * hd * 4 + tq * LANES * 4 + n_heads * t * 4)
            + r * hd * 2 + 2 * r * LANES * 4 + 2 * r * hd * 4 + 5 * r * tk * 4 + (6 << 20))
    return pl.pallas_call(
        functools.partial(_fox_prompt_kernel, tq=tq, tk=tk, scale=scale, hd=hd),
        grid=(n_seq, nq),
        in_specs=[pl.BlockSpec((tq, width), lambda b, i: (b * nq + i, 0)),
                  pl.BlockSpec((t, 2 * hd), lambda b, i: (b, 0)),
                  pl.BlockSpec((hd, t), lambda b, i: (0, b)),
                  pl.BlockSpec((t, n_heads), lambda b, i: (b, 0)),
                  pl.BlockSpec((None, n_heads, t), lambda b, i: (b, 0, 0))],
        out_specs=pl.BlockSpec((tq, width), lambda b, i: (b * nq + i, 0)),
        out_shape=jax.ShapeDtypeStruct((n_seq * t, width), MXU_DTYPE),
        scratch_shapes=[pltpu.VMEM((1, r), F32), pltpu.VMEM((1, r), F32), pltpu.VMEM((hd, r), F32),
                        pltpu.VMEM((tk, r), MXU_DTYPE)],
        compiler_params=_params(("parallel", "arbitrary"), vmem),
        name="fox_prompt",
    )(q, kv_rows, v_t, cum, cum_t)


def _cumsum_prompt_kernel(x_ref, o_ref, *, blk):
    t, c = x_ref.shape
    carry = jnp.zeros((1, c), F32)
    for s in range(0, t, blk):
        out = _prefix_sum_rows(x_ref[s:s + blk, :], carry)
        o_ref[s:s + blk, :] = out
        carry = out[blk - 1:blk, :]


def cumsum_prompt(logf, n_seq, t):
    c = logf.shape[1]
    blk = _tile(t, LANES)
    return pl.pallas_call(
        functools.partial(_cumsum_prompt_kernel, blk=blk),
        grid=(n_seq,),
        in_specs=[pl.BlockSpec((t, c), lambda b: (b, 0))],
        out_specs=pl.BlockSpec((t, c), lambda b: (b, 0)),
        out_shape=jax.ShapeDtypeStruct((n_seq * t, c), F32),
        compiler_params=_params(("parallel",), 16 << 20),
    )(logf)


def _dsa_prompt_kernel(qa_ref, qi_ref, wit_ref, rows_ref, vt_ref, bias_ref, o_ref,
                       score_ref, m_ref, l_ref, acc_ref, p_ref, *, tq, tk, topk, scale, w_scale, hd, idx_dim):
    i = pl.program_id(1)
    n_heads = qa_ref.shape[1] // hd
    n_idx = qi_ref.shape[1] // idx_dim
    n_chunks = (i * tq + tq + tk - 1) // tk

    score_ref[...] = jnp.full(score_ref.shape, -jnp.inf, F32)
    qi = _heads_to_rows(qi_ref[...].astype(MXU_DTYPE), n_idx, idx_dim)
    wi = wit_ref[...] * w_scale

    def score_step(kc, _):
        start = pl.multiple_of(kc * tk, tk)
        ki = rows_ref[pl.ds(start, tk), 2 * hd:2 * hd + idx_dim].astype(MXU_DTYPE)
        act = jnp.maximum(_nt_dot(ki, qi), 0.0)
        acc = jnp.zeros((tk, tq), F32)
        for h in range(n_idx):
            acc = acc + wi[h:h + 1, :] * act[:, h * tq:(h + 1) * tq]
        score_ref[pl.ds(start, tk), :] = jnp.where(_key_visible(kc, tk, i, tq), acc, -jnp.inf)
        return 0

    lax.fori_loop(0, n_chunks, score_step, 0)

    score_ref[...] = _topk_keep_mask(score_ref[...], topk, axis=0)

    q = _heads_to_rows(qa_ref[...].astype(MXU_DTYPE), n_heads, hd)
    _softmax_init(m_ref, l_ref, acc_ref)
    per_step = tk // LANES

    def attn_step(kb, _):
        start = pl.multiple_of(kb * tk, tk)
        k = rows_ref[pl.ds(start, tk), 0:hd].astype(MXU_DTYPE)
        v_t = vt_ref[:, pl.ds(start, tk)].astype(MXU_DTYPE)
        keep = score_ref[pl.ds(start, tk), :]
        raw = _nt_dot(k, q)
        behind = [jnp.clip(i - (kb * per_step + b), 0, 2) for b in range(per_step)]

        def logits(h):
            bias = jnp.concatenate([bias_ref[off, h] for off in behind], axis=0)
            return raw[:, h * tq:(h + 1) * tq] * scale + bias + keep

        _softmax_step_cols(logits, v_t, n_heads, tq, m_ref, l_ref, acc_ref, p_ref)
        return 0

    lax.fori_loop(0, n_chunks, attn_step, 0)
    _cols_to_heads(o_ref, acc_ref[...] / l_ref[...], n_heads, tq, hd)


def dsa_prompt(q_a, q_i, w_i_t, rows, v_t, bias_t, n_seq, t, scale, w_scale, hd, idx_dim):
    tq = LANES
    tk = 2 * LANES
    assert t % tk == 0
    nq = t // tq
    n_heads = q_a.shape[1] // hd
    r = n_heads * tq
    row_w = rows.shape[1]
    vmem = (2 * (2 * tq * q_a.shape[1] * 4 + tq * q_i.shape[1] * 4 + t * (row_w + hd) * 4 + bias_t.size * 4)
            + 4 * tq * t * 4 + 3 * r * hd * 4 + 3 * tk * q_i.shape[1] // idx_dim * tq * 4 + 3 * tk * r * 4 + (6 << 20))
    return pl.pallas_call(
        functools.partial(_dsa_prompt_kernel, tq=tq, tk=tk, topk=min(DSA_TOPK, t // 4), scale=scale,
                          w_scale=w_scale, hd=hd, idx_dim=idx_dim),
        grid=(n_seq, nq),
        in_specs=[pl.BlockSpec((tq, q_a.shape[1]), lambda b, i: (b * nq + i, 0)),
                  pl.BlockSpec((tq, q_i.shape[1]), lambda b, i: (b * nq + i, 0)),
                  pl.BlockSpec((w_i_t.shape[0], tq), lambda b, i: (0, b * nq + i)),
                  pl.BlockSpec((t, row_w), lambda b, i: (b, 0)),
                  pl.BlockSpec((hd, t), lambda b, i: (0, b)),
                  pl.BlockSpec(bias_t.shape, lambda b, i: (0, 0, 0, 0))],
        out_specs=pl.BlockSpec((tq, q_a.shape[1]), lambda b, i: (b * nq + i, 0)),
        out_shape=jax.ShapeDtypeStruct((n_seq * t, q_a.shape[1]), MXU_DTYPE),
        scratch_shapes=[pltpu.VMEM((t, tq), F32), pltpu.VMEM((1, r), F32), pltpu.VMEM((1, r), F32),
                        pltpu.VMEM((hd, r), F32), pltpu.VMEM((tk, r), MXU_DTYPE)],
        compiler_params=_params(("parallel", "arbitrary"), vmem),
        name="dsa_prompt",
    )(q_a, q_i, w_i_t, rows, v_t, bias_t)


def _page_specs(pool, layer, n_group):
    tile = pool.shape[2:]

    def spec(p):
        return pl.BlockSpec((None, None) + tile, lambda b, g, pt: (layer, pt[b, g * n_group + p], 0, 0))

    return [spec(p) for p in range(n_group)]


def _pad_rows(x, n):
    return jnp.concatenate([x, jnp.zeros((n - x.shape[0], x.shape[1]), x.dtype)], axis=0)


def _lane_pair(ref_a, ref_b, lo, hi):
    return jnp.concatenate([ref_a[lo:hi, :], ref_b[lo:hi, :]], axis=1).astype(MXU_DTYPE)


def _mla_sample_kernel(pt_ref, q_ref, new_ref, *rest, n_group, scale, dv):
    pages = rest[:n_group]
    o_ref, kt_ref, m_ref, l_ref, acc_ref = rest[n_group:]
    g = pl.program_id(1)
    n_heads, tq, dk = q_ref.shape
    rows = n_heads * tq
    width = 2 * pages[0].shape[1]
    q = q_ref[...].reshape(rows, dk).astype(MXU_DTYPE)

    @pl.when(g == 0)
    def _():
        _softmax_init(m_ref, l_ref, acc_ref)

    logits = []
    for pp in range(n_group // 2):
        kt = _lane_pair(pages[2 * pp], pages[2 * pp + 1], 0, dk)
        kt_ref[pp] = kt
        logits.append(_dot(q, kt) * scale)
    s = jnp.concatenate(logits, axis=1)
    m_old = m_ref[...]
    m_new = jnp.maximum(m_old, jnp.max(s, axis=1, keepdims=True))
    alpha = jnp.exp(m_old - m_new)
    prob = jnp.exp(s - m_new)
    l_ref[...] = alpha * l_ref[...] + jnp.sum(prob, axis=1, keepdims=True)
    m_ref[...] = m_new
    prob = prob.astype(MXU_DTYPE)
    pv = _nt_dot(prob[:, 0:width], kt_ref[0, 0:dv, :])
    for pp in range(1, n_group // 2):
        pv = pv + _nt_dot(prob[:, pp * width:(pp + 1) * width], kt_ref[pp, 0:dv, :])
    acc_ref[...] = alpha * acc_ref[...] + pv

    @pl.when(g == pl.num_programs(1) - 1)
    def _():
        k = _pad_rows(new_ref[...], LANES).astype(MXU_DTYPE)
        t_row = lax.broadcasted_iota(jnp.int32, (rows, 1), 0) % tq
        j = lax.broadcasted_iota(jnp.int32, (1, LANES), 1)
        s_new = jnp.where(j <= t_row, _nt_dot(q, k) * scale, MASK_VALUE)
        _softmax_step(s_new, k[:, :dv], m_ref, l_ref, acc_ref)
        _rows_to_heads(o_ref, acc_ref[...] / l_ref[...], n_heads, tq, dv)


def mla_sample(q_cat, rows, pool_t, layer, page_table, dv, row0, tq, scale, *, n_group=32):
    n_heads, _, dk = q_cat.shape
    n_seq, n_pages = page_table.shape
    page = pool_t.shape[3]
    n_group = _tile(n_pages, n_group, 2)
    assert n_group % 2 == 0
    qb0 = row0 // tq
    r = n_heads * tq
    grid_spec = pltpu.PrefetchScalarGridSpec(
        num_scalar_prefetch=1,
        grid=(n_seq, n_pages // n_group),
        in_specs=[pl.BlockSpec((n_heads, tq, dk), lambda b, g, pt: (0, qb0 + b, 0)),
                  pl.BlockSpec((tq, dk), lambda b, g, pt: (qb0 + b, 0))] + _page_specs(pool_t, layer, n_group),
        out_specs=pl.BlockSpec((tq, n_heads * dv), lambda b, g, pt: (b, 0)),
        scratch_shapes=[pltpu.VMEM((n_group // 2, dk, 2 * page), MXU_DTYPE),
                        pltpu.VMEM((r, 1), F32), pltpu.VMEM((r, 1), F32), pltpu.VMEM((r, dv), F32)],
    )
    vmem = n_group * dk * page * (2 * 4 + 2) + 4 * r * n_group * page * 4 + (12 << 20)
    return pl.pallas_call(
        functools.partial(_mla_sample_kernel, n_group=n_group, scale=scale, dv=dv),
        grid_spec=grid_spec,
        out_shape=jax.ShapeDtypeStruct((n_seq * tq, n_heads * dv), F32),
        compiler_params=_params(("parallel", "arbitrary"), vmem),
        name="mla_sample",
    )(page_table, q_cat, rows, *([pool_t] * n_group))


def _split3(x):
    hi = x.astype(MXU_DTYPE)
    r1 = x - hi.astype(F32)
    mid = r1.astype(MXU_DTYPE)
    lo = (r1 - mid.astype(F32)).astype(MXU_DTYPE)
    return hi, mid, lo


def _prefix_sum_lanes(blocks):
    c, n = blocks[0].shape
    tri = (lax.broadcasted_iota(jnp.int32, (n, n), 0) <= lax.broadcasted_iota(jnp.int32, (n, n), 1)).astype(MXU_DTYPE)
    pieces = [piece for x in blocks for piece in _split3(x)]
    out = _dot(jnp.concatenate(pieces, axis=0), tri)
    return [out[3 * i * c:(3 * i + 1) * c] + out[(3 * i + 1) * c:(3 * i + 2) * c]
            + out[(3 * i + 2) * c:(3 * i + 3) * c] for i in range(len(blocks))]


def _fox_sample_kernel(pt_ref, q_ref, new_ref, logf_new_ref, *rest, n_pages, scale, hd, batch):
    kv_pages = rest[:n_pages]
    lf_pages = rest[n_pages:2 * n_pages]
    o_ref, cum_ref, s_ref = rest[2 * n_pages:]
    tq = q_ref.shape[0]
    n_heads = q_ref.shape[1] // hd
    rows = n_heads * tq
    page = kv_pages[0].shape[0]
    past = n_pages * page

    carry = jnp.zeros((n_heads, 1), F32)
    for p0 in range(0, n_pages, batch):
        inner = _prefix_sum_lanes([lf_pages[p][...] for p in range(p0, p0 + batch)])
        for p, cum in zip(range(p0, p0 + batch), inner):
            cum_ref[:, p * page:(p + 1) * page] = cum + carry
            carry = carry + cum[:, page - 1:page]
    cum_new = _prefix_sum_lanes([logf_new_ref[...]])[0] + carry
    cum_ref[:, past:past + page] = cum_new
    pick = (lax.broadcasted_iota(jnp.int32, (tq, page), 0)
            == lax.broadcasted_iota(jnp.int32, (tq, page), 1)).astype(MXU_DTYPE)
    cum_q = sum(_nt_dot(pick, piece) for piece in _split3(cum_new))
    cum_q = jnp.stack([cum_q[:, h:h + 1] for h in range(n_heads)])

    q = _heads_to_rows(q_ref[...].astype(MXU_DTYPE), n_heads, hd)

    def logits(raw, cum_k):
        n = raw.shape[1]
        return (raw.reshape(n_heads, tq, n) * scale + cum_q - cum_k[:, None, :]).reshape(rows, n)

    for p in range(0, n_pages, 2):
        k = jnp.concatenate([kv_pages[p][:, 0:hd], kv_pages[p + 1][:, 0:hd]], axis=0).astype(MXU_DTYPE)
        s_ref[:, p * page:(p + 2) * page] = logits(_nt_dot(q, k), cum_ref[:, p * page:(p + 2) * page])
    kv_new = _pad_rows(new_ref[...], page).astype(MXU_DTYPE)
    t_row = lax.broadcasted_iota(jnp.int32, (rows, 1), 0) % tq
    j = lax.broadcasted_iota(jnp.int32, (1, page), 1)
    s_ref[:, past:past + page] = jnp.where(j <= t_row, logits(_nt_dot(q, kv_new[:, :hd]), cum_new), MASK_VALUE)

    s = s_ref[...]
    prob = jnp.exp(s - jnp.max(s, axis=1, keepdims=True))
    denom = jnp.sum(prob, axis=1, keepdims=True)
    prob = prob.astype(MXU_DTYPE)
    out = _dot(prob[:, past:past + page], kv_new[:, hd:])
    for p in range(0, n_pages, 2):
        v = jnp.concatenate([kv_pages[p][:, hd:], kv_pages[p + 1][:, hd:]], axis=0).astype(MXU_DTYPE)
        out = out + _dot(prob[:, p * page:(p + 2) * page], v)
    _rows_to_heads(o_ref, out / denom, n_heads, tq, hd)


def fox_sample(q, kv_new, logf_new_t, kv_pool, logf_pool_t, layer, page_table, row0, tq, scale):
    width = q.shape[1]
    hd = kv_new.shape[1] // 2
    n_heads = width // hd
    n_seq, n_pages = page_table.shape
    page = kv_pool.shape[2]
    assert n_pages % 2 == 0 and page == logf_pool_t.shape[3]
    qb0 = row0 // tq
    r = n_heads * tq
    n_keys = (n_pages + 1) * page
    grid_spec = pltpu.PrefetchScalarGridSpec(
        num_scalar_prefetch=1,
        grid=(n_seq, 1),
        in_specs=[pl.BlockSpec((tq, width), lambda b, g, pt: (qb0 + b, 0)),
                  pl.BlockSpec((tq, 2 * hd), lambda b, g, pt: (qb0 + b, 0)),
                  pl.BlockSpec((None, n_heads, page), lambda b, g, pt: (b, 0, 0))]
        + _page_specs(kv_pool, layer, n_pages) + _page_specs(logf_pool_t, layer, n_pages),
        out_specs=pl.BlockSpec((tq, width), lambda b, g, pt: (b, 0)),
        scratch_shapes=[pltpu.VMEM((n_heads, n_keys), F32), pltpu.VMEM((r, n_keys), F32)],
    )
    vmem = 2 * n_pages * page * (2 * hd + n_heads) * 4 + 4 * r * n_keys * 4 + (8 << 20)
    return pl.pallas_call(
        functools.partial(_fox_sample_kernel, n_pages=n_pages, scale=scale, hd=hd, batch=_tile(n_pages, 8, 1)),
        grid_spec=grid_spec,
        out_shape=jax.ShapeDtypeStruct((n_seq * tq, width), F32),
        compiler_params=_params(("parallel", "arbitrary"), vmem),
        name="fox_sample",
    )(page_table, q, kv_new, logf_new_t, *([kv_pool] * n_pages), *([logf_pool_t] * n_pages))


def _idx_scores(z, wi, n_idx, tq):
    z = jnp.maximum(z, 0.0)
    acc = jnp.zeros((tq, z.shape[1]), F32)
    for h in range(n_idx):
        acc = acc + wi[:, h:h + 1] * z[h * tq:(h + 1) * tq, :]
    return acc


def _dsa_sample_kernel(pt_ref, qa_ref, qi_ref, wi_ref, new_ref, bias_ref, *rest,
                       n_pages, topk, scale, w_scale, hd, idx_dim):
    pages = rest[:n_pages]
    o_ref, keep_ref, s_ref = rest[n_pages:]
    tq = qa_ref.shape[0]
    n_heads = qa_ref.shape[1] // hd
    n_idx = qi_ref.shape[1] // idx_dim
    rows = n_heads * tq
    page = pages[0].shape[1]
    past = n_pages * page
    new = new_ref[...]
    t_row = lax.broadcasted_iota(jnp.int32, (tq, 1), 0)
    j = lax.broadcasted_iota(jnp.int32, (1, page), 1)

    qi = _heads_to_rows(qi_ref[...].astype(MXU_DTYPE), n_idx, idx_dim)
    wi = wi_ref[...] * w_scale
    for p in range(0, n_pages, 2):
        ki_t = _lane_pair(pages[p], pages[p + 1], 2 * hd, 2 * hd + idx_dim)
        keep_ref[:, p * page:(p + 2) * page] = _idx_scores(_dot(qi, ki_t), wi, n_idx, tq)
    ki_new = _pad_rows(new[:, 2 * hd:2 * hd + idx_dim], page).astype(MXU_DTYPE)
    keep_ref[:, past:past + page] = jnp.where(j <= t_row, _idx_scores(_nt_dot(qi, ki_new), wi, n_idx, tq), -jnp.inf)

    keep_ref[...] = _topk_keep_mask(keep_ref[...], topk, axis=1)

    q = _heads_to_rows(qa_ref[...].astype(MXU_DTYPE), n_heads, hd)

    def logits(raw, bias, keep):
        n = raw.shape[1]
        return (raw.reshape(n_heads, tq, n) * scale + bias + keep[None]).reshape(rows, n)

    far = bias_ref[2]
    for p in range(0, n_pages, 2):
        k_t = _lane_pair(pages[p], pages[p + 1], 0, hd)
        bias = jnp.concatenate([far, bias_ref[1] if p + 2 == n_pages else far], axis=2)
        s_ref[:, p * page:(p + 2) * page] = logits(_dot(q, k_t), bias, keep_ref[:, p * page:(p + 2) * page])
    kv_new = _pad_rows(new[:, 0:2 * hd], page).astype(MXU_DTYPE)
    s_ref[:, past:past + page] = logits(_nt_dot(q, kv_new[:, :hd]), bias_ref[0], keep_ref[:, past:past + page])

    s = s_ref[...]
    prob = jnp.exp(s - jnp.max(s, axis=1, keepdims=True))
    denom = jnp.sum(prob, axis=1, keepdims=True)
    prob = prob.astype(MXU_DTYPE)
    out = _dot(prob[:, past:past + page], kv_new[:, hd:])
    for p in range(0, n_pages, 2):
        out = out + _nt_dot(prob[:, p * page:(p + 2) * page], _lane_pair(pages[p], pages[p + 1], hd, 2 * hd))
    _rows_to_heads(o_ref, out / denom, n_heads, tq, hd)


def dsa_sample(q_a, q_i, w_i, rows, bias, pool_t, layer, page_table, row0, tq, scale, w_scale, hd, idx_dim):
    n_seq, n_pages = page_table.shape
    page = pool_t.shape[3]
    assert page == LANES and n_pages % 2 == 0
    qb0 = row0 // tq
    width = q_a.shape[1]
    n_keys = (n_pages + 1) * page
    r = (width // hd) * tq
    grid_spec = pltpu.PrefetchScalarGridSpec(
        num_scalar_prefetch=1,
        grid=(n_seq, 1),
        in_specs=[pl.BlockSpec((tq, width), lambda b, g, pt: (qb0 + b, 0)),
                  pl.BlockSpec((tq, q_i.shape[1]), lambda b, g, pt: (qb0 + b, 0)),
                  pl.BlockSpec((tq, w_i.shape[1]), lambda b, g, pt: (qb0 + b, 0)),
                  pl.BlockSpec((tq, rows.shape[1]), lambda b, g, pt: (qb0 + b, 0)),
                  pl.BlockSpec(bias.shape, lambda b, g, pt: (0, 0, 0, 0))] + _page_specs(pool_t, layer, n_pages),
        out_specs=pl.BlockSpec((tq, width), lambda b, g, pt: (b, 0)),
        scratch_shapes=[pltpu.VMEM((tq, n_keys), F32), pltpu.VMEM((r, n_keys), F32)],
    )
    vmem = 2 * n_pages * pool_t.shape[2] * page * 4 + 4 * r * n_keys * 4 + (10 << 20)
    return pl.pallas_call(
        functools.partial(_dsa_sample_kernel, n_pages=n_pages, topk=min(DSA_TOPK, (n_pages * page + tq) // 4),
                          scale=scale, w_scale=w_scale, hd=hd, idx_dim=idx_dim),
        grid_spec=grid_spec,
        out_shape=jax.ShapeDtypeStruct((n_seq * tq, width), F32),
        compiler_params=_params(("parallel", "arbitrary"), vmem),
        name="dsa_sample",
    )(page_table, q_a, q_i, w_i, rows, bias, *([pool_t] * n_pages))


def kernel(x_prompt, x_sample, cache_dsa, cache_mla, cache_fox_kv, cache_fox_logf, page_table, p_prompt, p_sample,
           rel_bias, w_in_ab, mla_q_norm, w_uq, mla_kv_norm, w_uk, w_uv, w_o_ab, w_in_c, b_forget, w_o_c,
           ln1_g, ln1_b, ln2_g, ln2_b, w_ffn_up, w_ffn_down, w_router, w_moe_up, w_moe_down,
           w_ple_gate, b_ple_gate, w_ple_proj):
    n_p, t_p, d = x_prompt.shape
    n_s, t_s, _ = x_sample.shape
    depth = ln1_g.shape[0]
    m_p, m_s = n_p * t_p, n_s * t_s
    hd = cache_fox_kv.shape[-1] // 2
    idx_dim = cache_dsa.shape[-1] - 2 * hd
    kv_lora = w_uk.shape[1]
    rope = cache_mla.shape[-1] - kv_lora
    q_lora = w_uq.shape[1]
    nope = w_uk.shape[3]
    h_a = rel_bias.shape[1]
    h_c = b_forget.shape[1]
    ab_cols = w_in_ab.shape[2]
    n_idx_w = (ab_cols - h_a * hd - 2 * hd - idx_dim - q_lora - kv_lora - rope) // (idx_dim + 1)
    past_len = page_table.shape[1] * cache_dsa.shape[2]
    alpha = (2 * depth) ** 0.25
    attn_scale = hd ** -0.5
    mla_scale = (nope + rope) ** -0.5
    idx_w_scale = (n_idx_w * idx_dim) ** -0.5

    c_qa = 0
    c_kv = h_a * hd
    c_qi = c_kv + 2 * hd
    c_tail = c_qi + n_idx_w * idx_dim
    t_wi = idx_dim
    t_cq = t_wi + n_idx_w
    t_ckv = t_cq + q_lora
    t_kr = t_ckv + kv_lora

    x = jnp.concatenate([x_prompt.reshape(m_p, d), x_sample.reshape(m_s, d)], axis=0)
    xb = x.astype(MXU_DTYPE)
    ple = jnp.concatenate([p_prompt.reshape(depth, m_p, -1), p_sample.reshape(depth, m_s, -1)], axis=1)
    pos = jnp.concatenate([jnp.tile(jnp.arange(t_p, dtype=jnp.int32), n_p),
                           past_len + jnp.tile(jnp.arange(t_s, dtype=jnp.int32), n_s)])
    inv = ROPE_THETA ** (-jnp.arange(0, rope, 2, dtype=F32) / rope)
    ang = pos.astype(F32)[:, None] * inv[None, :]
    cos, sin = jnp.cos(ang), jnp.sin(ang)
    bias_p_t = jnp.swapaxes(bias_tables(rel_bias, LANES), 2, 3)
    bias_s = bias_tables(rel_bias, t_s)
    dsa_pool_t = jnp.swapaxes(cache_dsa, 2, 3)
    mla_pool_t = jnp.swapaxes(cache_mla, 2, 3)
    logf_pool_t = jnp.swapaxes(cache_fox_logf, 2, 3)
    w_in_ab_t = jnp.swapaxes(w_in_ab, 1, 2)
    w_in_c_t = jnp.swapaxes(w_in_c, 1, 2)

    new_dsa, new_mla, new_fkv, new_flf = [], [], [], []
    for i in range(depth):
        j = i // 2
        if i % 2 == 0:
            q_a = matmul(xb, w_in_ab_t, layer=j, transposed=True, out_dtype=F32, col_start=c_qa, n_cols=h_a * hd)
            kv_a = matmul(xb, w_in_ab_t, layer=j, transposed=True, out_dtype=F32, col_start=c_kv, n_cols=2 * hd)
            q_i = matmul(xb, w_in_ab_t, layer=j, transposed=True, out_dtype=F32, col_start=c_qi,
                         n_cols=n_idx_w * idx_dim)
            tail = matmul(xb, w_in_ab_t[j, c_tail:, :], transposed=True, out_dtype=F32)
            w_i = tail[:, t_wi:t_cq]
            cqn, mla_rows = ab_post(tail[:, t_cq:t_ckv], tail[:, t_ckv:t_kr], tail[:, t_kr:], cos, sin,
                                    mla_q_norm[j], mla_kv_norm[j])
            dsa_rows = jnp.concatenate([kv_a, tail[:, :t_wi]], axis=-1)
            q_cat = mla_queries(cqn, w_uq[j], w_uk[j], cos, sin)
            w_uv_heads = jnp.transpose(w_uv[j], (1, 0, 2))

            o_a_p = dsa_prompt(q_a, q_i, jnp.transpose(w_i), dsa_rows, jnp.transpose(kv_a[:, hd:]), bias_p_t,
                               n_p, t_p, attn_scale, idx_w_scale, hd, idx_dim)
            o_a_s = dsa_sample(q_a, q_i, w_i, dsa_rows, bias_s, dsa_pool_t, j, page_table, m_p, t_s,
                               attn_scale, idx_w_scale, hd, idx_dim)
            o_b_p = mla_out(mla_prompt(q_cat, mla_rows, jnp.transpose(mla_rows[:, :kv_lora]), n_p, t_p, mla_scale),
                            w_uv_heads)
            o_b_s = mla_out(mla_sample(q_cat, mla_rows, mla_pool_t, j, page_table, kv_lora, m_p, t_s, mla_scale),
                            w_uv_heads)
            heads = jnp.concatenate([jnp.concatenate([o_a_p, o_a_s.astype(MXU_DTYPE)], axis=0),
                                     jnp.concatenate([o_b_p, o_b_s], axis=0)], axis=1)
            mix = matmul(heads, w_o_ab, layer=j, out_dtype=F32)
            new_dsa.append(dsa_rows)
            new_mla.append(mla_rows)
        else:
            q = matmul(xb, w_in_c_t, layer=j, transposed=True, out_dtype=F32, col_start=0, n_cols=h_c * hd)
            kv_c = matmul(xb, w_in_c_t, layer=j, transposed=True, out_dtype=F32, col_start=h_c * hd, n_cols=2 * hd)
            logf = fox_log_forget(xb, w_in_c[j, :, h_c * hd + 2 * hd:], b_forget[j])
            cum_p = cumsum_prompt(logf[:m_p], n_p, t_p)
            cum_t_p = jnp.swapaxes(cum_p.reshape(n_p, t_p, h_c), 1, 2)
            o_p = fox_prompt(q, kv_c, jnp.transpose(kv_c[:, hd:]), cum_p, cum_t_p, n_p, t_p, attn_scale)
            logf_new_t = jnp.pad(jnp.swapaxes(logf[m_p:].reshape(n_s, t_s, h_c), 1, 2),
                                 ((0, 0), (0, 0), (0, logf_pool_t.shape[3] - t_s)))
            o_s = fox_sample(q, kv_c, logf_new_t, cache_fox_kv, logf_pool_t, j, page_table, m_p, t_s, attn_scale)
            heads = jnp.concatenate([o_p, o_s.astype(MXU_DTYPE)], axis=0)
            mix = matmul(heads, w_o_c, layer=j, out_dtype=F32)
            new_fkv.append(kv_c)
            new_flf.append(logf)
        h, hb = deepnorm(x, mix, ln1_g[i], ln1_b[i], alpha)
        if i % 2 == 0:
            ff = matmul(swiglu_up(hb, w_ffn_up, j), w_ffn_down, layer=j, out_dtype=F32)
        else:
            combine = router_combine(hb, w_router[j])
            ff = moe_down(moe_up(hb, w_moe_up, j), w_moe_down, j, jnp.transpose(combine)[:, :, None])
        h, hb = deepnorm(h, ff, ln2_g[i], ln2_b[i], alpha)
        x, xb = ple_mix(h, hb, w_ple_gate, b_ple_gate[i], ple, w_ple_proj, i)

    def split(rows_list):
        a = jnp.stack(rows_list)
        return a[:, :m_p].reshape(len(rows_list), n_p, t_p, -1), a[:, m_p:].reshape(len(rows_list), n_s, t_s, -1)

    dsa_p, dsa_s = split(new_dsa)
    mla_p, mla_s = split(new_mla)
    fkv_p, fkv_s = split(new_fkv)
    flf_p, flf_s = split(new_flf)
    return (x[:m_p].reshape(n_p, t_p, d), x[m_p:].reshape(n_s, t_s, d),
            dsa_p, mla_p, fkv_p, flf_p, dsa_s, mla_s, fkv_s, flf_s)
```

```python
import functools
import math

import numpy as np
import jax
import jax.numpy as jnp
from jax import lax
from jax.experimental import pallas as pl
from jax.experimental.pallas import tpu as pltpu

DSA_TOPK = 256
N_BUCKETS = 32
MAX_DISTANCE = 128
ROPE_THETA = 10000.0
LN_EPS = 1e-5
RMS_EPS = 1e-6
MOE_TOP_K = 2

MXU_DTYPE = jnp.bfloat16
MASK_VALUE = -1e30
V7X_VMEM_BYTES = 64 << 20
VMEM_CAP_BYTES = V7X_VMEM_BYTES - (6 << 20)
LANES = 128
F32 = jnp.float32


def _tile(n, target, align=8):
    if n <= target:
        return n
    for d in range(target, 0, -1):
        if n % d == 0 and d % align == 0:
            return d
    return n


def _params(semantics, vmem_bytes):
    limit = int(min(max(vmem_bytes, 16 << 20), VMEM_CAP_BYTES))
    return pltpu.CompilerParams(dimension_semantics=semantics, vmem_limit_bytes=limit)


def _nt_dot(a, b):
    return lax.dot_general(a, b, (((1,), (1,)), ((), ())), preferred_element_type=F32)


def _dot(a, b):
    return jnp.dot(a, b, preferred_element_type=F32)


def _mm_kernel(x_ref, w_ref, o_ref, *scratch, nk, transposed):
    contract = _nt_dot if transposed else _dot
    part = contract(x_ref[...].astype(MXU_DTYPE), w_ref[...].astype(MXU_DTYPE))
    if nk == 1:
        o_ref[...] = part.astype(o_ref.dtype)
        return
    acc_ref, = scratch
    k = pl.program_id(2)

    @pl.when(k == 0)
    def _():
        acc_ref[...] = part

    @pl.when(k > 0)
    def _():
        acc_ref[...] += part

    @pl.when(k == nk - 1)
    def _():
        o_ref[...] = acc_ref[...].astype(o_ref.dtype)


def _mm_stationary_kernel(x_ref, w_ref, o_ref, wb_ref, *, transposed):
    @pl.when(pl.program_id(1) == 0)
    def _():
        wb_ref[...] = w_ref[...].astype(wb_ref.dtype)

    contract = _nt_dot if transposed else _dot
    o_ref[...] = contract(x_ref[...].astype(MXU_DTYPE), wb_ref[...]).astype(o_ref.dtype)


def matmul(x, w, *, out_dtype, layer=None, transposed=False, col_start=0, n_cols=None, tm=1024, tn=256, tk=8192):
    m, kdim = x.shape
    n_total = w.shape[-2] if transposed else w.shape[-1]
    n_cols = n_total - col_start if n_cols is None else n_cols
    tm = _tile(m, tm)
    tk = _tile(kdim, tk, LANES)
    if n_cols % LANES == 0 and col_start % LANES == 0:
        tn = _tile(n_cols, tn, LANES)
        while col_start % tn:
            tn -= LANES
    else:
        assert col_start == 0 and n_cols == n_total
        tn = n_cols
    off = col_start // tn
    xb, ob = x.dtype.itemsize, jnp.dtype(out_dtype).itemsize
    cast_x = 0 if x.dtype == MXU_DTYPE else 2

    tn_pad = -(-tn // LANES) * LANES

    def estimate(tk):
        return (2 * (tm * tk * xb + tk * tn_pad * 4 + tm * tn_pad * ob) + 2 * tm * tn_pad * 4 + tk * tn_pad * 2
                + tm * tk * cast_x + (4 << 20))

    while estimate(tk) > VMEM_CAP_BYTES * 7 // 8 and tk % (2 * LANES) == 0:
        tk //= 2
    nk = kdim // tk
    vmem = estimate(tk)
    w_block = (tn, tk) if transposed else (tk, tn)

    def w_index(i, j, k):
        pos = (j + off, k) if transposed else (k, j + off)
        return pos if layer is None else (layer,) + pos

    w_spec = pl.BlockSpec(w_block if layer is None else (None,) + w_block, w_index)
    if nk == 1:
        return pl.pallas_call(
            functools.partial(_mm_stationary_kernel, transposed=transposed),
            grid=(n_cols // tn, m // tm),
            in_specs=[pl.BlockSpec((tm, tk), lambda j, i: (i, 0)),
                      pl.BlockSpec(w_spec.block_shape, lambda j, i: w_index(i, j, 0))],
            out_specs=pl.BlockSpec((tm, tn), lambda j, i: (i, j)),
            out_shape=jax.ShapeDtypeStruct((m, n_cols), out_dtype),
            scratch_shapes=[pltpu.VMEM(w_block, MXU_DTYPE)],
            compiler_params=_params(("arbitrary", "arbitrary"), vmem),
            name="matmul_ws",
        )(x, w)
    return pl.pallas_call(
        functools.partial(_mm_kernel, nk=nk, transposed=transposed),
        grid=(m // tm, n_cols // tn, nk),
        in_specs=[pl.BlockSpec((tm, tk), lambda i, j, k: (i, k)), w_spec],
        out_specs=pl.BlockSpec((tm, tn), lambda i, j, k: (i, j)),
        out_shape=jax.ShapeDtypeStruct((m, n_cols), out_dtype),
        scratch_shapes=[pltpu.VMEM((tm, tn), F32)] if nk > 1 else [],
        compiler_params=_params(("parallel", "parallel", "arbitrary"), vmem),
        name="matmul",
    )(x, w)


def _swiglu_up_kernel(x_ref, wg_ref, wu_ref, o_ref):
    x = x_ref[...]
    gate = _dot(x, wg_ref[...].astype(MXU_DTYPE))
    up = _dot(x, wu_ref[...].astype(MXU_DTYPE))
    o_ref[...] = (gate * jax.nn.sigmoid(gate) * up).astype(o_ref.dtype)


def swiglu_up(xb, w_up, layer, *, tm=1024, tn=256):
    m, d = xb.shape
    f = w_up.shape[2] // 2
    tm, tn = _tile(m, tm), _tile(f, tn, LANES)
    nf = f // tn
    vmem = 2 * (tm * d * 2 + 2 * d * tn * 4 + tm * tn * 2) + 2 * d * tn * 2 + 3 * tm * tn * 4 + (4 << 20)
    return pl.pallas_call(
        _swiglu_up_kernel,
        grid=(m // tm, nf),
        in_specs=[pl.BlockSpec((tm, d), lambda i, j: (i, 0)),
                  pl.BlockSpec((None, d, tn), lambda i, j: (layer, 0, j)),
                  pl.BlockSpec((None, d, tn), lambda i, j: (layer, 0, j + nf))],
        out_specs=pl.BlockSpec((tm, tn), lambda i, j: (i, j)),
        out_shape=jax.ShapeDtypeStruct((m, f), MXU_DTYPE),
        compiler_params=_params(("parallel", "parallel"), vmem),
        name="swiglu_up",
    )(xb, w_up, w_up)


def moe_up(xb, w_up, layer, *, tm=1024, tn=256):
    m, d = xb.shape
    _, n_exp, _, f2 = w_up.shape
    f = f2 // 2
    tm, tn = _tile(m, tm), _tile(f, tn, LANES)
    nf = f // tn
    vmem = 2 * (tm * d * 2 + 2 * d * tn * 4 + tm * tn * 2) + 2 * d * tn * 2 + 3 * tm * tn * 4 + (4 << 20)
    return pl.pallas_call(
        _swiglu_up_kernel,
        grid=(m // tm, n_exp, nf),
        in_specs=[pl.BlockSpec((tm, d), lambda i, e, j: (i, 0)),
                  pl.BlockSpec((None, None, d, tn), lambda i, e, j: (layer, e, 0, j)),
                  pl.BlockSpec((None, None, d, tn), lambda i, e, j: (layer, e, 0, j + nf))],
        out_specs=pl.BlockSpec((tm, tn), lambda i, e, j: (i, e * nf + j)),
        out_shape=jax.ShapeDtypeStruct((m, n_exp * f), MXU_DTYPE),
        compiler_params=_params(("parallel", "parallel", "parallel"), vmem),
        name="moe_up",
    )(xb, w_up, w_up)


def _moe_down_kernel(a_ref, w_ref, c_ref, o_ref, acc_ref, *, n_exp):
    e = pl.program_id(2)
    part = c_ref[...] * _dot(a_ref[...], w_ref[...].astype(MXU_DTYPE))

    @pl.when(e == 0)
    def _():
        acc_ref[...] = part

    @pl.when(e > 0)
    def _():
        acc_ref[...] += part

    @pl.when(e == n_exp - 1)
    def _():
        o_ref[...] = acc_ref[...]


def moe_down(act, w_down, layer, combine_t, *, tm=1024, tn=1024):
    m = act.shape[0]
    _, n_exp, f, d = w_down.shape
    tm, tn = _tile(m, tm), _tile(d, tn, LANES)
    vmem = 2 * (tm * f * 2 + f * tn * 4 + tm * LANES * 4 + tm * tn * 4) + 3 * tm * tn * 4 + f * tn * 2 + (4 << 20)
    return pl.pallas_call(
        functools.partial(_moe_down_kernel, n_exp=n_exp),
        grid=(m // tm, d // tn, n_exp),
        in_specs=[pl.BlockSpec((tm, f), lambda i, j, e: (i, e)),
                  pl.BlockSpec((None, None, f, tn), lambda i, j, e: (layer, e, 0, j)),
                  pl.BlockSpec((None, tm, 1), lambda i, j, e: (e, i, 0))],
        out_specs=pl.BlockSpec((tm, tn), lambda i, j, e: (i, j)),
        out_shape=jax.ShapeDtypeStruct((m, d), F32),
        scratch_shapes=[pltpu.VMEM((tm, tn), F32)],
        compiler_params=_params(("parallel", "parallel", "arbitrary"), vmem),
        name="moe_down",
    )(act, w_down, combine_t)


def _router_kernel(x_ref, w_ref, o_ref):
    logits = _dot(x_ref[...], w_ref[...].astype(MXU_DTYPE))
    n_exp = logits.shape[1]
    lane = lax.broadcasted_iota(jnp.int32, logits.shape, 1)
    m1 = jnp.max(logits, axis=1, keepdims=True)
    i1 = jnp.min(jnp.where(logits == m1, lane, n_exp), axis=1, keepdims=True)
    rest = jnp.where(lane == i1, -jnp.inf, logits)
    m2 = jnp.max(rest, axis=1, keepdims=True)
    i2 = jnp.min(jnp.where(rest == m2, lane, n_exp), axis=1, keepdims=True)
    e2 = jnp.exp(m2 - m1)
    denom = 1.0 + e2
    o_ref[...] = jnp.where(lane == i1, 1.0 / denom, 0.0) + jnp.where(lane == i2, e2 / denom, 0.0)


def router_combine(xb, w_router, *, tm=1024):
    m, d = xb.shape
    n_exp = w_router.shape[1]
    tm = _tile(m, tm)
    vmem = 2 * (tm * d * 2 + d * LANES * 4 + tm * LANES * 4) + (8 << 20)
    return pl.pallas_call(
        _router_kernel,
        grid=(m // tm,),
        in_specs=[pl.BlockSpec((tm, d), lambda i: (i, 0)),
                  pl.BlockSpec((d, n_exp), lambda i: (0, 0))],
        out_specs=pl.BlockSpec((tm, n_exp), lambda i: (i, 0)),
        out_shape=jax.ShapeDtypeStruct((m, n_exp), F32),
        compiler_params=_params(("parallel",), vmem),
    )(xb, w_router)


def _deepnorm_kernel(x_ref, y_ref, g_ref, b_ref, o_ref, ob_ref, *, alpha):
    z = alpha * x_ref[...] + y_ref[...]
    mu = jnp.mean(z, axis=-1, keepdims=True)
    zc = z - mu
    var = jnp.mean(zc * zc, axis=-1, keepdims=True)
    out = zc * lax.rsqrt(var + LN_EPS) * g_ref[...] + b_ref[...]
    o_ref[...] = out
    ob_ref[...] = out.astype(ob_ref.dtype)


def deepnorm(x, y, g, b, alpha, *, tm=256):
    m, d = x.shape
    tm = _tile(m, tm)
    row = pl.BlockSpec((tm, d), lambda i: (i, 0))
    vec = pl.BlockSpec((1, d), lambda i: (0, 0))
    return pl.pallas_call(
        functools.partial(_deepnorm_kernel, alpha=alpha),
        grid=(m // tm,),
        in_specs=[row, row, vec, vec],
        out_specs=[row, row],
        out_shape=[jax.ShapeDtypeStruct((m, d), F32), jax.ShapeDtypeStruct((m, d), MXU_DTYPE)],
        compiler_params=_params(("parallel",), 12 * tm * d * 4 + (4 << 20)),
    )(x, y, g.reshape(1, d), b.reshape(1, d))


def _ple_kernel(hb_ref, wg_ref, bg_ref, p_ref, wp_ref, h_ref, o_ref, ob_ref):
    gate = jax.nn.sigmoid(_dot(hb_ref[...], wg_ref[...].astype(MXU_DTYPE)) + bg_ref[...])
    emb = _dot(p_ref[...].astype(MXU_DTYPE), wp_ref[...].astype(MXU_DTYPE))
    out = h_ref[...] + gate * emb
    o_ref[...] = out
    ob_ref[...] = out.astype(ob_ref.dtype)


def ple_mix(h, hb, w_gate, b_gate, p, w_proj, layer, *, tm=1024, tn=256):
    m, d = h.shape
    pd = p.shape[2]
    tm, tn = _tile(m, tm), _tile(d, tn, LANES)
    vmem = 2 * (tm * d * 2 + d * tn * 4 + tm * pd * 4 + pd * tn * 4 + 3 * tm * tn * 4) + d * tn * 2 + 4 * tm * tn * 4 + (4 << 20)
    tile = pl.BlockSpec((tm, tn), lambda i, j: (i, j))
    return pl.pallas_call(
        _ple_kernel,
        grid=(m // tm, d // tn),
        in_specs=[pl.BlockSpec((tm, d), lambda i, j: (i, 0)),
                  pl.BlockSpec((None, d, tn), lambda i, j: (layer, 0, j)),
                  pl.BlockSpec((1, tn), lambda i, j: (0, j)),
                  pl.BlockSpec((None, tm, pd), lambda i, j: (layer, i, 0)),
                  pl.BlockSpec((None, pd, tn), lambda i, j: (layer, 0, j)),
                  tile],
        out_specs=[tile, tile],
        out_shape=[jax.ShapeDtypeStruct((m, d), F32), jax.ShapeDtypeStruct((m, d), MXU_DTYPE)],
        compiler_params=_params(("parallel", "parallel"), vmem),
        name="ple_mix",
    )(hb, w_gate, b_gate.reshape(1, d), p, w_proj, h)


def _rms(x, g):
    return x * lax.rsqrt(jnp.mean(x * x, axis=-1, keepdims=True) + RMS_EPS) * g


def _ab_post_kernel(cq_ref, ckv_ref, k1_ref, k2_ref, cos_ref, sin_ref, gq_ref, gkv_ref,
                    cqn_ref, ckvn_ref, r1_ref, r2_ref):
    cqn_ref[...] = _rms(cq_ref[...], gq_ref[...]).astype(cqn_ref.dtype)
    ckvn_ref[...] = _rms(ckv_ref[...], gkv_ref[...])
    x1, x2, cos, sin = k1_ref[...], k2_ref[...], cos_ref[...], sin_ref[...]
    r1_ref[...] = x1 * cos - x2 * sin
    r2_ref[...] = x1 * sin + x2 * cos


def ab_post(c_q, c_kv, k_r, cos, sin, q_norm, kv_norm, *, tm=512):
    m, ql = c_q.shape
    kl, half = c_kv.shape[1], cos.shape[1]
    tm = _tile(m, tm)

    def row(n):
        return pl.BlockSpec((tm, n), lambda i: (i, 0))

    def vec(n):
        return pl.BlockSpec((1, n), lambda i: (0, 0))

    cqn, ckvn, r1, r2 = pl.pallas_call(
        _ab_post_kernel,
        grid=(m // tm,),
        in_specs=[row(ql), row(kl), row(half), row(half), row(half), row(half), vec(ql), vec(kl)],
        out_specs=[row(ql), row(kl), row(half), row(half)],
        out_shape=[jax.ShapeDtypeStruct((m, ql), MXU_DTYPE), jax.ShapeDtypeStruct((m, kl), F32),
                   jax.ShapeDtypeStruct((m, half), F32), jax.ShapeDtypeStruct((m, half), F32)],
        compiler_params=_params(("parallel",), 32 << 20),
    )(c_q, c_kv, k_r[:, :half], k_r[:, half:], cos, sin, q_norm.reshape(1, ql), kv_norm.reshape(1, kl))
    return cqn, jnp.concatenate([ckvn, r1, r2], axis=-1)


def _mla_q_kernel(c_ref, wn_ref, w1_ref, w2_ref, wk_ref, cos_ref, sin_ref, o_ref, *, kl, half):
    c = c_ref[...]
    q_nope = _dot(c, wn_ref[...].astype(MXU_DTYPE))
    x1 = _dot(c, w1_ref[...].astype(MXU_DTYPE))
    x2 = _dot(c, w2_ref[...].astype(MXU_DTYPE))
    cos, sin = cos_ref[...], sin_ref[...]
    o_ref[:, 0:kl] = _nt_dot(q_nope.astype(MXU_DTYPE), wk_ref[...].astype(MXU_DTYPE))
    o_ref[:, kl:kl + half] = x1 * cos - x2 * sin
    o_ref[:, kl + half:kl + 2 * half] = x1 * sin + x2 * cos


def mla_queries(cqn, w_uq, w_uk, cos, sin, *, tm=1024):
    m, ql = cqn.shape
    kl, n_heads, nope = w_uk.shape
    half = cos.shape[1]
    tm = _tile(m, tm)
    w_heads = jnp.transpose(w_uq, (1, 0, 2))
    wk_heads = jnp.transpose(w_uk, (1, 0, 2))
    return pl.pallas_call(
        functools.partial(_mla_q_kernel, kl=kl, half=half),
        grid=(m // tm, n_heads),
        in_specs=[pl.BlockSpec((tm, ql), lambda i, h: (i, 0)),
                  pl.BlockSpec((None, ql, nope), lambda i, h: (h, 0, 0)),
                  pl.BlockSpec((None, ql, half), lambda i, h: (h, 0, 0)),
                  pl.BlockSpec((None, ql, half), lambda i, h: (h, 0, 0)),
                  pl.BlockSpec((None, kl, nope), lambda i, h: (h, 0, 0)),
                  pl.BlockSpec((tm, half), lambda i, h: (i, 0)),
                  pl.BlockSpec((tm, half), lambda i, h: (i, 0))],
        out_specs=pl.BlockSpec((None, tm, kl + 2 * half), lambda i, h: (h, i, 0)),
        out_shape=jax.ShapeDtypeStruct((n_heads, m, kl + 2 * half), F32),
        compiler_params=_params(("parallel", "parallel"), 32 << 20),
    )(cqn, w_heads[:, :, :nope], w_heads[:, :, nope:nope + half], w_heads[:, :, nope + half:], wk_heads, cos, sin)


def _fox_gate_kernel(x_ref, w_ref, b_ref, o_ref):
    z = _dot(x_ref[...], w_ref[...].astype(MXU_DTYPE)) + b_ref[...]
    o_ref[...] = jnp.minimum(z, 0.0) - jnp.log1p(jnp.exp(-jnp.abs(z)))


def fox_log_forget(xb, w_f, b_f, *, tm=1024):
    m, d = xb.shape
    h = w_f.shape[1]
    tm = _tile(m, tm)
    return pl.pallas_call(
        _fox_gate_kernel,
        grid=(m // tm,),
        in_specs=[pl.BlockSpec((tm, d), lambda i: (i, 0)),
                  pl.BlockSpec((d, h), lambda i: (0, 0)),
                  pl.BlockSpec((1, h), lambda i: (0, 0))],
        out_specs=pl.BlockSpec((tm, h), lambda i: (i, 0)),
        out_shape=jax.ShapeDtypeStruct((m, h), F32),
        compiler_params=_params(("parallel",), 2 * (tm * d * 2 + d * LANES * 4 + tm * LANES * 4) + (8 << 20)),
    )(xb, w_f, b_f.reshape(1, h))


def _softmax_init(m_ref, l_ref, acc_ref):
    m_ref[...] = jnp.full(m_ref.shape, MASK_VALUE, F32)
    l_ref[...] = jnp.zeros(l_ref.shape, F32)
    acc_ref[...] = jnp.zeros(acc_ref.shape, F32)


def _softmax_step(s, v, m_ref, l_ref, acc_ref):
    m_old = m_ref[...]
    m_new = jnp.maximum(m_old, jnp.max(s, axis=1, keepdims=True))
    alpha = jnp.exp(m_old - m_new)
    p = jnp.exp(s - m_new)
    l_ref[...] = alpha * l_ref[...] + jnp.sum(p, axis=1, keepdims=True)
    acc_ref[...] = alpha * acc_ref[...] + _dot(p.astype(MXU_DTYPE), v)
    m_ref[...] = m_new


def _softmax_step_cols(logits_of_head, v_t, n_heads, tq, m_ref, l_ref, acc_ref, p_ref):
    for h in range(n_heads):
        c = slice(h * tq, (h + 1) * tq)
        s = logits_of_head(h)
        m_old = m_ref[:, c]
        m_new = jnp.maximum(m_old, jnp.max(s, axis=0, keepdims=True))
        alpha = jnp.exp(m_old - m_new)
        p = jnp.exp(s - m_new)
        l_ref[:, c] = alpha * l_ref[:, c] + jnp.sum(p, axis=0, keepdims=True)
        m_ref[:, c] = m_new
        acc_ref[:, c] = alpha * acc_ref[:, c]
        p_ref[:, c] = p.astype(p_ref.dtype)
    acc_ref[...] += _dot(v_t, p_ref[...])


def _cols_to_heads(o_ref, o_t, n_heads, tq, dv):
    for h in range(n_heads):
        o_ref[:, h * dv:(h + 1) * dv] = o_t[:, h * tq:(h + 1) * tq].T.astype(o_ref.dtype)


def _heads_to_rows(q, n_heads, hd):
    return jnp.concatenate([q[:, h * hd:(h + 1) * hd] for h in range(n_heads)], axis=0)


def _rows_to_heads(o_ref, o, n_heads, t, hd):
    for h in range(n_heads):
        o_ref[:, h * hd:(h + 1) * hd] = o[h * t:(h + 1) * t, :].astype(o_ref.dtype)


def _prefix_sum_rows(x, carry):
    n = x.shape[0]
    tri = (lax.broadcasted_iota(jnp.int32, (n, n), 1) <= lax.broadcasted_iota(jnp.int32, (n, n), 0)).astype(MXU_DTYPE)
    hi = x.astype(MXU_DTYPE)
    r1 = x - hi.astype(F32)
    mid = r1.astype(MXU_DTYPE)
    lo = (r1 - mid.astype(F32)).astype(MXU_DTYPE)
    return _dot(tri, hi) + _dot(tri, mid) + _dot(tri, lo) + carry


def _order_key(score):
    bits = lax.bitcast_convert_type(score, jnp.int32)
    return jnp.where(bits < 0, bits ^ jnp.int32(0x7FFFFFFF), bits)


def _count(mask, axis):
    step = 8 if axis == 0 else LANES
    hit = jnp.where(mask, 1.0, 0.0)
    parts = [lax.slice_in_dim(hit, s, s + step, axis=axis) for s in range(0, mask.shape[axis], step)]
    while len(parts) > 1:
        parts = [a + b for a, b in zip(parts[0::2], parts[1::2])] + parts[len(parts) - len(parts) % 2:]
    return jnp.sum(parts[0], axis=axis, keepdims=True)


def _kth_largest_key(key, k, axis):
    shape = tuple(1 if a == axis else n for a, n in enumerate(key.shape))
    int_min = jnp.int32(-2 ** 31)
    thr = jnp.where(_count(key >= jnp.zeros(shape, jnp.int32), axis) >= k, jnp.int32(0), int_min)

    def single(thr, bit):
        cand = thr | bit
        return jnp.where(_count(key >= cand, axis) >= k, cand, thr)

    def body(it, thr):
        hi = lax.shift_left(jnp.int32(1), jnp.int32(30) - 2 * it)
        lo = lax.shift_right_logical(hi, jnp.int32(1))
        c_hi, c_lo, c_both = thr | hi, thr | lo, thr | hi | lo
        n_hi, n_lo, n_both = _count(key >= c_hi, axis), _count(key >= c_lo, axis), _count(key >= c_both, axis)
        return jnp.where(n_both >= k, c_both, jnp.where(n_hi >= k, c_hi, jnp.where(n_lo >= k, c_lo, thr)))

    if key.size <= 128 * 8 * LANES:
        thr = lax.fori_loop(0, 15, body, thr)
        return single(thr, jnp.int32(1))
    return lax.fori_loop(0, 31, lambda it, t: single(t, lax.shift_left(jnp.int32(1), jnp.int32(30) - it)), thr)


def _running_count(mask, axis):
    n = mask.shape[axis]
    ones = jnp.where(mask, 1.0, 0.0)
    if axis == 0:
        blk = _tile(n, 2 * LANES, LANES)
        tri = (lax.broadcasted_iota(jnp.int32, (blk, blk), 1)
               <= lax.broadcasted_iota(jnp.int32, (blk, blk), 0)).astype(MXU_DTYPE)
        carry = jnp.zeros((1, mask.shape[1]), F32)
        out = []
        for s in range(0, n, blk):
            part = ones[s:s + blk, :]
            incl = _dot(tri, part.astype(MXU_DTYPE))
            out.append(incl - part + carry)
            carry = carry + incl[blk - 1:blk, :]
        return jnp.concatenate(out, axis=0)
    r = mask.shape[0]
    tri = (lax.broadcasted_iota(jnp.int32, (LANES, LANES), 0)
           <= lax.broadcasted_iota(jnp.int32, (LANES, LANES), 1)).astype(MXU_DTYPE)
    parts = [ones[:, s:s + LANES] for s in range(0, n, LANES)]
    incl = _dot(jnp.concatenate(parts, axis=0).astype(MXU_DTYPE), tri)
    carry = jnp.zeros((r, 1), F32)
    out = []
    for b, part in enumerate(parts):
        inc = incl[b * r:(b + 1) * r, :]
        out.append(inc - part + carry)
        carry = carry + inc[:, LANES - 1:LANES]
    return jnp.concatenate(out, axis=1)


def _topk_keep_mask(score, k, axis):
    key = _order_key(score)
    thr = _kth_largest_key(key, k, axis)
    above = key > thr
    tied = key == thr
    need = k - _count(above, axis)
    keep = (above | (tied & (_running_count(tied, axis) < need))) & (score > -jnp.inf)
    return jnp.where(keep, 0.0, MASK_VALUE)


def bias_tables(rel_bias, tq):
    max_exact = N_BUCKETS // 2
    d = np.arange(0, 2 * LANES + tq)
    ratio = np.log(np.maximum(d, 1).astype(np.float32) / max_exact) / math.log(MAX_DISTANCE / max_exact)
    large = np.minimum(max_exact + (ratio * (N_BUCKETS - max_exact)).astype(np.int32), N_BUCKETS - 1)
    bucket = np.where(d < max_exact, d, large)
    i = np.arange(tq)[:, None]
    j = np.arange(LANES)[None, :]
    dist = np.stack([np.maximum(i - j, 0), i - j + LANES, np.full((tq, LANES), 2 * LANES)])
    one_hot = (bucket[dist][..., None] == np.arange(N_BUCKETS)).astype(np.float32)
    return jnp.einsum("abcn,nh->ahbc", one_hot, rel_bias.astype(F32), precision=lax.Precision.HIGHEST)


def _mla_out_kernel(o_ref, w_ref, out_ref):
    out_ref[...] = _dot(o_ref[...].astype(MXU_DTYPE), w_ref[...].astype(MXU_DTYPE)).astype(out_ref.dtype)


def mla_out(o_lat, w_uv_heads, *, tm=1024):
    m = o_lat.shape[0]
    n_heads, dv, hd = w_uv_heads.shape
    tm = _tile(m, tm)
    return pl.pallas_call(
        _mla_out_kernel,
        grid=(m // tm, n_heads),
        in_specs=[pl.BlockSpec((tm, dv), lambda i, h: (i, h)),
                  pl.BlockSpec((None, dv, hd), lambda i, h: (h, 0, 0))],
        out_specs=pl.BlockSpec((tm, hd), lambda i, h: (i, h)),
        out_shape=jax.ShapeDtypeStruct((m, n_heads * hd), MXU_DTYPE),
        compiler_params=_params(("parallel", "parallel"), 24 << 20),
        name="mla_out",
    )(o_lat, w_uv_heads)


def _causal_chunks(i, tq, tk, step):
    n_full = (i * tq + 1) // tk

    def body(kb, carry):
        step(kb, False)
        return carry

    lax.fori_loop(0, n_full, body, 0)
    step(n_full, True)


def _key_visible(kb, tk, i, tq):
    k_pos = kb * tk + lax.broadcasted_iota(jnp.int32, (tk, 1), 0)
    q_pos = i * tq + lax.broadcasted_iota(jnp.int32, (1, tq), 1)
    return k_pos <= q_pos


def _mla_prompt_kernel(q_ref, k_ref, vt_ref, o_ref, m_ref, l_ref, acc_ref, p_ref, *, tq, tk, scale):
    i = pl.program_id(1)
    n_heads, _, dk = q_ref.shape
    dv = vt_ref.shape[0]
    q = q_ref[...].reshape(n_heads * tq, dk).astype(MXU_DTYPE)
    _softmax_init(m_ref, l_ref, acc_ref)

    def step(kb, masked):
        start = pl.multiple_of(kb * tk, tk)
        k = k_ref[pl.ds(start, tk), :].astype(MXU_DTYPE)
        v_t = vt_ref[:, pl.ds(start, tk)].astype(MXU_DTYPE)
        raw = _nt_dot(k, q)
        visible = _key_visible(kb, tk, i, tq)

        def logits(h):
            s = raw[:, h * tq:(h + 1) * tq] * scale
            return jnp.where(visible, s, MASK_VALUE) if masked else s

        _softmax_step_cols(logits, v_t, n_heads, tq, m_ref, l_ref, acc_ref, p_ref)

    _causal_chunks(i, tq, tk, step)
    _cols_to_heads(o_ref, acc_ref[...] / l_ref[...], n_heads, tq, dv)


def mla_prompt(q_cat, rows, v_t, n_seq, t, scale, *, tq=128, tk=256):
    n_heads, _, dk = q_cat.shape
    dv = v_t.shape[0]
    tq, tk = _tile(t, tq), _tile(t, tk)
    assert tk % tq == 0
    nq = t // tq
    r = n_heads * tq
    vmem = (2 * (r * dk * 4 + t * dk * 4 + t * dv * 4 + tq * n_heads * dv * 2)
            + r * dk * 2 + 3 * r * dv * 4 + 4 * r * tk * 4 + (6 << 20))
    return pl.pallas_call(
        functools.partial(_mla_prompt_kernel, tq=tq, tk=tk, scale=scale),
        grid=(n_seq, nq),
        in_specs=[pl.BlockSpec((n_heads, tq, dk), lambda b, i: (0, b * nq + i, 0)),
                  pl.BlockSpec((t, dk), lambda b, i: (b, 0)),
                  pl.BlockSpec((dv, t), lambda b, i: (0, b))],
        out_specs=pl.BlockSpec((tq, n_heads * dv), lambda b, i: (b * nq + i, 0)),
        out_shape=jax.ShapeDtypeStruct((n_seq * t, n_heads * dv), MXU_DTYPE),
        scratch_shapes=[pltpu.VMEM((1, r), F32), pltpu.VMEM((1, r), F32), pltpu.VMEM((dv, r), F32),
                        pltpu.VMEM((tk, r), MXU_DTYPE)],
        compiler_params=_params(("parallel", "arbitrary"), vmem),
        name="mla_prompt",
    )(q_cat, rows, v_t)


def _fox_prompt_kernel(q_ref, kv_ref, vt_ref, cum_ref, cumt_ref, o_ref, m_ref, l_ref, acc_ref, p_ref,
                       *, tq, tk, scale, hd):
    i = pl.program_id(1)
    n_heads = q_ref.shape[1] // hd
    q = _heads_to_rows(q_ref[...].astype(MXU_DTYPE), n_heads, hd)
    cum_q = cumt_ref[:, pl.ds(pl.multiple_of(i * tq, tq), tq)]
    _softmax_init(m_ref, l_ref, acc_ref)

    def step(kb, masked):
        start = pl.multiple_of(kb * tk, tk)
        k = kv_ref[pl.ds(start, tk), 0:hd].astype(MXU_DTYPE)
        v_t = vt_ref[:, pl.ds(start, tk)].astype(MXU_DTYPE)
        cum_k = cum_ref[pl.ds(start, tk), :]
        raw = _nt_dot(k, q)
        visible = _key_visible(kb, tk, i, tq)

        def logits(h):
            s = raw[:, h * tq:(h + 1) * tq] * scale + cum_q[h:h + 1, :] - cum_k[:, h:h + 1]
            return jnp.where(visible, s, MASK_VALUE) if masked else s

        _softmax_step_cols(logits, v_t, n_heads, tq, m_ref, l_ref, acc_ref, p_ref)

    _causal_chunks(i, tq, tk, step)
    _cols_to_heads(o_ref, acc_ref[...] / l_ref[...], n_heads, tq, hd)


def fox_prompt(q, kv_rows, v_t, cum, cum_t, n_seq, t, scale, *, tq=128, tk=256):
    width = q.shape[1]
    hd = kv_rows.shape[1] // 2
    n_heads = width // hd
    tq, tk = _tile(t, tq), _tile(t, tk)
    assert tk % tq == 0
    nq = t // tq
    r = n_heads * tq
    vmem = (2 * (2 * tq * width * 2 + t * 2 * hd * 4 + tq * LANES * 4 + n_heads * t * 4)
            + r * hd * 2 + 2 * r * LANES * 4 + 2 * r * hd * 4 + 5 * r * tk * 4 + (6 << 20))
    return pl.pallas_call(
        functools.partial(_fox_prompt_kernel, tq=tq, tk=tk, scale=scale, hd=hd),
        grid=(n_seq, nq),
        in_specs=[pl.BlockSpec((tq, width), lambda b, i: (b * nq + i, 0)),
                  pl.BlockSpec((t, 2 * hd), lambda b, i: (b, 0)),
                  pl.BlockSpec((hd, t), lambda b, i: (0, b)),
                  pl.BlockSpec((t, n_heads), lambda b, i: (b, 0)),
                  pl.BlockSpec((None, n_heads, t), lambda b, i: (b, 0, 0))],
        out_specs=pl.BlockSpec((tq, width), lambda b, i: (b * nq + i, 0)),
        out_shape=jax.ShapeDtypeStruct((n_seq * t, width), MXU_DTYPE),
        scratch_shapes=[pltpu.VMEM((1, r), F32), pltpu.VMEM((1, r), F32), pltpu.VMEM((hd, r), F32),
                        pltpu.VMEM((tk, r), MXU_DTYPE)],
        compiler_params=_params(("parallel", "arbitrary"), vmem),
        name="fox_prompt",
    )(q, kv_rows, v_t, cum, cum_t)


def _cumsum_prompt_kernel(x_ref, o_ref, *, blk):
    t, c = x_ref.shape
    carry = jnp.zeros((1, c), F32)
    for s in range(0, t, blk):
        out = _prefix_sum_rows(x_ref[s:s + blk, :], carry)
        o_ref[s:s + blk, :] = out
        carry = out[blk - 1:blk, :]


def cumsum_prompt(logf, n_seq, t):
    c = logf.shape[1]
    blk = _tile(t, LANES)
    return pl.pallas_call(
        functools.partial(_cumsum_prompt_kernel, blk=blk),
        grid=(n_seq,),
        in_specs=[pl.BlockSpec((t, c), lambda b: (b, 0))],
        out_specs=pl.BlockSpec((t, c), lambda b: (b, 0)),
        out_shape=jax.ShapeDtypeStruct((n_seq * t, c), F32),
        compiler_params=_params(("parallel",), 16 << 20),
    )(logf)


def _dsa_prompt_kernel(qa_ref, qi_ref, wit_ref, rows_ref, vt_ref, bias_ref, o_ref,
                       score_ref, m_ref, l_ref, acc_ref, p_ref, *, tq, tk, topk, scale, w_scale, hd, idx_dim):
    i = pl.program_id(1)
    n_heads = qa_ref.shape[1] // hd
    n_idx = qi_ref.shape[1] // idx_dim
    n_chunks = (i * tq + tq + tk - 1) // tk

    score_ref[...] = jnp.full(score_ref.shape, -jnp.inf, F32)
    qi = _heads_to_rows(qi_ref[...].astype(MXU_DTYPE), n_idx, idx_dim)
    wi = wit_ref[...] * w_scale

    def score_step(kc, _):
        start = pl.multiple_of(kc * tk, tk)
        ki = rows_ref[pl.ds(start, tk), 2 * hd:2 * hd + idx_dim].astype(MXU_DTYPE)
        act = jnp.maximum(_nt_dot(ki, qi), 0.0)
        acc = jnp.zeros((tk, tq), F32)
        for h in range(n_idx):
            acc = acc + wi[h:h + 1, :] * act[:, h * tq:(h + 1) * tq]
        score_ref[pl.ds(start, tk), :] = jnp.where(_key_visible(kc, tk, i, tq), acc, -jnp.inf)
        return 0

    lax.fori_loop(0, n_chunks, score_step, 0)

    score_ref[...] = _topk_keep_mask(score_ref[...], topk, axis=0)

    q = _heads_to_rows(qa_ref[...].astype(MXU_DTYPE), n_heads, hd)
    _softmax_init(m_ref, l_ref, acc_ref)
    per_step = tk // LANES

    def attn_step(kb, _):
        start = pl.multiple_of(kb * tk, tk)
        k = rows_ref[pl.ds(start, tk), 0:hd].astype(MXU_DTYPE)
        v_t = vt_ref[:, pl.ds(start, tk)].astype(MXU_DTYPE)
        keep = score_ref[pl.ds(start, tk), :]
        raw = _nt_dot(k, q)
        behind = [jnp.clip(i - (kb * per_step + b), 0, 2) for b in range(per_step)]

        def logits(h):
            bias = jnp.concatenate([bias_ref[off, h] for off in behind], axis=0)
            return raw[:, h * tq:(h + 1) * tq] * scale + bias + keep

        _softmax_step_cols(logits, v_t, n_heads, tq, m_ref, l_ref, acc_ref, p_ref)
        return 0

    lax.fori_loop(0, n_chunks, attn_step, 0)
    _cols_to_heads(o_ref, acc_ref[...] / l_ref[...], n_heads, tq, hd)


def dsa_prompt(q_a, q_i, w_i_t, rows, v_t, bias_t, n_seq, t, scale, w_scale, hd, idx_dim):
    tq = LANES
    tk = 2 * LANES
    assert t % tk == 0
    nq = t // tq
    n_heads = q_a.shape[1] // hd
    r = n_heads * tq
    row_w = rows.shape[1]
    vmem = (2 * (2 * tq * q_a.shape[1] * 4 + tq * q_i.shape[1] * 4 + t * (row_w + hd) * 4 + bias_t.size * 4)
            + 4 * tq * t * 4 + 3 * r * hd * 4 + 3 * tk * q_i.shape[1] // idx_dim * tq * 4 + 3 * tk * r * 4 + (6 << 20))
    return pl.pallas_call(
        functools.partial(_dsa_prompt_kernel, tq=tq, tk=tk, topk=min(DSA_TOPK, t // 4), scale=scale,
                          w_scale=w_scale, hd=hd, idx_dim=idx_dim),
        grid=(n_seq, nq),
        in_specs=[pl.BlockSpec((tq, q_a.shape[1]), lambda b, i: (b * nq + i, 0)),
                  pl.BlockSpec((tq, q_i.shape[1]), lambda b, i: (b * nq + i, 0)),
                  pl.BlockSpec((w_i_t.shape[0], tq), lambda b, i: (0, b * nq + i)),
                  pl.BlockSpec((t, row_w), lambda b, i: (b, 0)),
                  pl.BlockSpec((hd, t), lambda b, i: (0, b)),
                  pl.BlockSpec(bias_t.shape, lambda b, i: (0, 0, 0, 0))],
        out_specs=pl.BlockSpec((tq, q_a.shape[1]), lambda b, i: (b * nq + i, 0)),
        out_shape=jax.ShapeDtypeStruct((n_seq * t, q_a.shape[1]), MXU_DTYPE),
        scratch_shapes=[pltpu.VMEM((t, tq), F32), pltpu.VMEM((1, r), F32), pltpu.VMEM((1, r), F32),
                        pltpu.VMEM((hd, r), F32), pltpu.VMEM((tk, r), MXU_DTYPE)],
        compiler_params=_params(("parallel", "arbitrary"), vmem),
        name="dsa_prompt",
    )(q_a, q_i, w_i_t, rows, v_t, bias_t)


def _page_specs(pool, layer, n_group):
    tile = pool.shape[2:]

    def spec(p):
        return pl.BlockSpec((None, None) + tile, lambda b, g, pt: (layer, pt[b, g * n_group + p], 0, 0))

    return [spec(p) for p in range(n_group)]


def _pad_rows(x, n):
    return jnp.concatenate([x, jnp.zeros((n - x.shape[0], x.shape[1]), x.dtype)], axis=0)


def _lane_pair(ref_a, ref_b, lo, hi):
    return jnp.concatenate([ref_a[lo:hi, :], ref_b[lo:hi, :]], axis=1).astype(MXU_DTYPE)


def _mla_sample_kernel(pt_ref, q_ref, new_ref, *rest, n_group, scale, dv):
    pages = rest[:n_group]
    o_ref, kt_ref, m_ref, l_ref, acc_ref = rest[n_group:]
    g = pl.program_id(1)
    n_heads, tq, dk = q_ref.shape
    rows = n_heads * tq
    width = 2 * pages[0].shape[1]
    q = q_ref[...].reshape(rows, dk).astype(MXU_DTYPE)

    @pl.when(g == 0)
    def _():
        _softmax_init(m_ref, l_ref, acc_ref)

    logits = []
    for pp in range(n_group // 2):
        kt = _lane_pair(pages[2 * pp], pages[2 * pp + 1], 0, dk)
        kt_ref[pp] = kt
        logits.append(_dot(q, kt) * scale)
    s = jnp.concatenate(logits, axis=1)
    m_old = m_ref[...]
    m_new = jnp.maximum(m_old, jnp.max(s, axis=1, keepdims=True))
    alpha = jnp.exp(m_old - m_new)
    prob = jnp.exp(s - m_new)
    l_ref[...] = alpha * l_ref[...] + jnp.sum(prob, axis=1, keepdims=True)
    m_ref[...] = m_new
    prob = prob.astype(MXU_DTYPE)
    pv = _nt_dot(prob[:, 0:width], kt_ref[0, 0:dv, :])
    for pp in range(1, n_group // 2):
        pv = pv + _nt_dot(prob[:, pp * width:(pp + 1) * width], kt_ref[pp, 0:dv, :])
    acc_ref[...] = alpha * acc_ref[...] + pv

    @pl.when(g == pl.num_programs(1) - 1)
    def _():
        k = _pad_rows(new_ref[...], LANES).astype(MXU_DTYPE)
        t_row = lax.broadcasted_iota(jnp.int32, (rows, 1), 0) % tq
        j = lax.broadcasted_iota(jnp.int32, (1, LANES), 1)
        s_new = jnp.where(j <= t_row, _nt_dot(q, k) * scale, MASK_VALUE)
        _softmax_step(s_new, k[:, :dv], m_ref, l_ref, acc_ref)
        _rows_to_heads(o_ref, acc_ref[...] / l_ref[...], n_heads, tq, dv)


def mla_sample(q_cat, rows, pool_t, layer, page_table, dv, row0, tq, scale, *, n_group=32):
    n_heads, _, dk = q_cat.shape
    n_seq, n_pages = page_table.shape
    page = pool_t.shape[3]
    n_group = _tile(n_pages, n_group, 2)
    assert n_group % 2 == 0
    qb0 = row0 // tq
    r = n_heads * tq
    grid_spec = pltpu.PrefetchScalarGridSpec(
        num_scalar_prefetch=1,
        grid=(n_seq, n_pages // n_group),
        in_specs=[pl.BlockSpec((n_heads, tq, dk), lambda b, g, pt: (0, qb0 + b, 0)),
                  pl.BlockSpec((tq, dk), lambda b, g, pt: (qb0 + b, 0))] + _page_specs(pool_t, layer, n_group),
        out_specs=pl.BlockSpec((tq, n_heads * dv), lambda b, g, pt: (b, 0)),
        scratch_shapes=[pltpu.VMEM((n_group // 2, dk, 2 * page), MXU_DTYPE),
                        pltpu.VMEM((r, 1), F32), pltpu.VMEM((r, 1), F32), pltpu.VMEM((r, dv), F32)],
    )
    vmem = n_group * dk * page * (2 * 4 + 2) + 4 * r * n_group * page * 4 + (12 << 20)
    return pl.pallas_call(
        functools.partial(_mla_sample_kernel, n_group=n_group, scale=scale, dv=dv),
        grid_spec=grid_spec,
        out_shape=jax.ShapeDtypeStruct((n_seq * tq, n_heads * dv), F32),
        compiler_params=_params(("parallel", "arbitrary"), vmem),
        name="mla_sample",
    )(page_table, q_cat, rows, *([pool_t] * n_group))


def _split3(x):
    hi = x.astype(MXU_DTYPE)
    r1 = x - hi.astype(F32)
    mid = r1.astype(MXU_DTYPE)
    lo = (r1 - mid.astype(F32)).astype(MXU_DTYPE)
    return hi, mid, lo


def _prefix_sum_lanes(blocks):
    c, n = blocks[0].shape
    tri = (lax.broadcasted_iota(jnp.int32, (n, n), 0) <= lax.broadcasted_iota(jnp.int32, (n, n), 1)).astype(MXU_DTYPE)
    pieces = [piece for x in blocks for piece in _split3(x)]
    out = _dot(jnp.concatenate(pieces, axis=0), tri)
    return [out[3 * i * c:(3 * i + 1) * c] + out[(3 * i + 1) * c:(3 * i + 2) * c]
            + out[(3 * i + 2) * c:(3 * i + 3) * c] for i in range(len(blocks))]


def _fox_sample_kernel(pt_ref, q_ref, new_ref, logf_new_ref, *rest, n_pages, scale, hd, batch):
    kv_pages = rest[:n_pages]
    lf_pages = rest[n_pages:2 * n_pages]
    o_ref, cum_ref, s_ref = rest[2 * n_pages:]
    tq = q_ref.shape[0]
    n_heads = q_ref.shape[1] // hd
    rows = n_heads * tq
    page = kv_pages[0].shape[0]
    past = n_pages * page

    carry = jnp.zeros((n_heads, 1), F32)
    for p0 in range(0, n_pages, batch):
        inner = _prefix_sum_lanes([lf_pages[p][...] for p in range(p0, p0 + batch)])
        for p, cum in zip(range(p0, p0 + batch), inner):
            cum_ref[:, p * page:(p + 1) * page] = cum + carry
            carry = carry + cum[:, page - 1:page]
    cum_new = _prefix_sum_lanes([logf_new_ref[...]])[0] + carry
    cum_ref[:, past:past + page] = cum_new
    pick = (lax.broadcasted_iota(jnp.int32, (tq, page), 0)
            == lax.broadcasted_iota(jnp.int32, (tq, page), 1)).astype(MXU_DTYPE)
    cum_q = sum(_nt_dot(pick, piece) for piece in _split3(cum_new))
    cum_q = jnp.stack([cum_q[:, h:h + 1] for h in range(n_heads)])

    q = _heads_to_rows(q_ref[...].astype(MXU_DTYPE), n_heads, hd)

    def logits(raw, cum_k):
        n = raw.shape[1]
        return (raw.reshape(n_heads, tq, n) * scale + cum_q - cum_k[:, None, :]).reshape(rows, n)

    for p in range(0, n_pages, 2):
        k = jnp.concatenate([kv_pages[p][:, 0:hd], kv_pages[p + 1][:, 0:hd]], axis=0).astype(MXU_DTYPE)
        s_ref[:, p * page:(p + 2) * page] = logits(_nt_dot(q, k), cum_ref[:, p * page:(p + 2) * page])
    kv_new = _pad_rows(new_ref[...], page).astype(MXU_DTYPE)
    t_row = lax.broadcasted_iota(jnp.int32, (rows, 1), 0) % tq
    j = lax.broadcasted_iota(jnp.int32, (1, page), 1)
    s_ref[:, past:past + page] = jnp.where(j <= t_row, logits(_nt_dot(q, kv_new[:, :hd]), cum_new), MASK_VALUE)

    s = s_ref[...]
    prob = jnp.exp(s - jnp.max(s, axis=1, keepdims=True))
    denom = jnp.sum(prob, axis=1, keepdims=True)
    prob = prob.astype(MXU_DTYPE)
    out = _dot(prob[:, past:past + page], kv_new[:, hd:])
    for p in range(0, n_pages, 2):
        v = jnp.concatenate([kv_pages[p][:, hd:], kv_pages[p + 1][:, hd:]], axis=0).astype(MXU_DTYPE)
        out = out + _dot(prob[:, p * page:(p + 2) * page], v)
    _rows_to_heads(o_ref, out / denom, n_heads, tq, hd)


def fox_sample(q, kv_new, logf_new_t, kv_pool, logf_pool_t, layer, page_table, row0, tq, scale):
    width = q.shape[1]
    hd = kv_new.shape[1] // 2
    n_heads = width // hd
    n_seq, n_pages = page_table.shape
    page = kv_pool.shape[2]
    assert n_pages % 2 == 0 and page == logf_pool_t.shape[3]
    qb0 = row0 // tq
    r = n_heads * tq
    n_keys = (n_pages + 1) * page
    grid_spec = pltpu.PrefetchScalarGridSpec(
        num_scalar_prefetch=1,
        grid=(n_seq, 1),
        in_specs=[pl.BlockSpec((tq, width), lambda b, g, pt: (qb0 + b, 0)),
                  pl.BlockSpec((tq, 2 * hd), lambda b, g, pt: (qb0 + b, 0)),
                  pl.BlockSpec((None, n_heads, page), lambda b, g, pt: (b, 0, 0))]
        + _page_specs(kv_pool, layer, n_pages) + _page_specs(logf_pool_t, layer, n_pages),
        out_specs=pl.BlockSpec((tq, width), lambda b, g, pt: (b, 0)),
        scratch_shapes=[pltpu.VMEM((n_heads, n_keys), F32), pltpu.VMEM((r, n_keys), F32)],
    )
    vmem = 2 * n_pages * page * (2 * hd + n_heads) * 4 + 4 * r * n_keys * 4 + (8 << 20)
    return pl.pallas_call(
        functools.partial(_fox_sample_kernel, n_pages=n_pages, scale=scale, hd=hd, batch=_tile(n_pages, 8, 1)),
        grid_spec=grid_spec,
        out_shape=jax.ShapeDtypeStruct((n_seq * tq, width), F32),
        compiler_params=_params(("parallel", "arbitrary"), vmem),
        name="fox_sample",
    )(page_table, q, kv_new, logf_new_t, *([kv_pool] * n_pages), *([logf_pool_t] * n_pages))


def _idx_scores(z, wi, n_idx, tq):
    z = jnp.maximum(z, 0.0)
    acc = jnp.zeros((tq, z.shape[1]), F32)
    for h in range(n_idx):
        acc = acc + wi[:, h:h + 1] * z[h * tq:(h + 1) * tq, :]
    return acc


def _dsa_sample_kernel(pt_ref, qa_ref, qi_ref, wi_ref, new_ref, bias_ref, *rest,
                       n_pages, topk, scale, w_scale, hd, idx_dim):
    pages = rest[:n_pages]
    o_ref, keep_ref, s_ref = rest[n_pages:]
    tq = qa_ref.shape[0]
    n_heads = qa_ref.shape[1] // hd
    n_idx = qi_ref.shape[1] // idx_dim
    rows = n_heads * tq
    page = pages[0].shape[1]
    past = n_pages * page
    new = new_ref[...]
    t_row = lax.broadcasted_iota(jnp.int32, (tq, 1), 0)
    j = lax.broadcasted_iota(jnp.int32, (1, page), 1)

    qi = _heads_to_rows(qi_ref[...].astype(MXU_DTYPE), n_idx, idx_dim)
    wi = wi_ref[...] * w_scale
    for p in range(0, n_pages, 2):
        ki_t = _lane_pair(pages[p], pages[p + 1], 2 * hd, 2 * hd + idx_dim)
        keep_ref[:, p * page:(p + 2) * page] = _idx_scores(_dot(qi, ki_t), wi, n_idx, tq)
    ki_new = _pad_rows(new[:, 2 * hd:2 * hd + idx_dim], page).astype(MXU_DTYPE)
    keep_ref[:, past:past + page] = jnp.where(j <= t_row, _idx_scores(_nt_dot(qi, ki_new), wi, n_idx, tq), -jnp.inf)

    keep_ref[...] = _topk_keep_mask(keep_ref[...], topk, axis=1)

    q = _heads_to_rows(qa_ref[...].astype(MXU_DTYPE), n_heads, hd)

    def logits(raw, bias, keep):
        n = raw.shape[1]
        return (raw.reshape(n_heads, tq, n) * scale + bias + keep[None]).reshape(rows, n)

    far = bias_ref[2]
    for p in range(0, n_pages, 2):
        k_t = _lane_pair(pages[p], pages[p + 1], 0, hd)
        bias = jnp.concatenate([far, bias_ref[1] if p + 2 == n_pages else far], axis=2)
        s_ref[:, p * page:(p + 2) * page] = logits(_dot(q, k_t), bias, keep_ref[:, p * page:(p + 2) * page])
    kv_new = _pad_rows(new[:, 0:2 * hd], page).astype(MXU_DTYPE)
    s_ref[:, past:past + page] = logits(_nt_dot(q, kv_new[:, :hd]), bias_ref[0], keep_ref[:, past:past + page])

    s = s_ref[...]
    prob = jnp.exp(s - jnp.max(s, axis=1, keepdims=True))
    denom = jnp.sum(prob, axis=1, keepdims=True)
    prob = prob.astype(MXU_DTYPE)
    out = _dot(prob[:, past:past + page], kv_new[:, hd:])
    for p in range(0, n_pages, 2):
        out = out + _nt_dot(prob[:, p * page:(p + 2) * page], _lane_pair(pages[p], pages[p + 1], hd, 2 * hd))
    _rows_to_heads(o_ref, out / denom, n_heads, tq, hd)


def dsa_sample(q_a, q_i, w_i, rows, bias, pool_t, layer, page_table, row0, tq, scale, w_scale, hd, idx_dim):
    n_seq, n_pages = page_table.shape
    page = pool_t.shape[3]
    assert page == LANES and n_pages % 2 == 0
    qb0 = row0 // tq
    width = q_a.shape[1]
    n_keys = (n_pages + 1) * page
    r = (width // hd) * tq
    grid_spec = pltpu.PrefetchScalarGridSpec(
        num_scalar_prefetch=1,
        grid=(n_seq, 1),
        in_specs=[pl.BlockSpec((tq, width), lambda b, g, pt: (qb0 + b, 0)),
                  pl.BlockSpec((tq, q_i.shape[1]), lambda b, g, pt: (qb0 + b, 0)),
                  pl.BlockSpec((tq, w_i.shape[1]), lambda b, g, pt: (qb0 + b, 0)),
                  pl.BlockSpec((tq, rows.shape[1]), lambda b, g, pt: (qb0 + b, 0)),
                  pl.BlockSpec(bias.shape, lambda b, g, pt: (0, 0, 0, 0))] + _page_specs(pool_t, layer, n_pages),
        out_specs=pl.BlockSpec((tq, width), lambda b, g, pt: (b, 0)),
        scratch_shapes=[pltpu.VMEM((tq, n_keys), F32), pltpu.VMEM((r, n_keys), F32)],
    )
    vmem = 2 * n_pages * pool_t.shape[2] * page * 4 + 4 * r * n_keys * 4 + (10 << 20)
    return pl.pallas_call(
        functools.partial(_dsa_sample_kernel, n_pages=n_pages, topk=min(DSA_TOPK, (n_pages * page + tq) // 4),
                          scale=scale, w_scale=w_scale, hd=hd, idx_dim=idx_dim),
        grid_spec=grid_spec,
        out_shape=jax.ShapeDtypeStruct((n_seq * tq, width), F32),
        compiler_params=_params(("parallel", "arbitrary"), vmem),
        name="dsa_sample",
    )(page_table, q_a, q_i, w_i, rows, bias, *([pool_t] * n_pages))


def kernel(x_prompt, x_sample, cache_dsa, cache_mla, cache_fox_kv, cache_fox_logf, page_table, p_prompt, p_sample,
           rel_bias, w_in_ab, mla_q_norm, w_uq, mla_kv_norm, w_uk, w_uv, w_o_ab, w_in_c, b_forget, w_o_c,
           ln1_g, ln1_b, ln2_g, ln2_b, w_ffn_up, w_ffn_down, w_router, w_moe_up, w_moe_down,
           w_ple_gate, b_ple_gate, w_ple_proj):
    n_p, t_p, d = x_prompt.shape
    n_s, t_s, _ = x_sample.shape
    depth = ln1_g.shape[0]
    m_p, m_s = n_p * t_p, n_s * t_s
    hd = cache_fox_kv.shape[-1] // 2
    idx_dim = cache_dsa.shape[-1] - 2 * hd
    kv_lora = w_uk.shape[1]
    rope = cache_mla.shape[-1] - kv_lora
    q_lora = w_uq.shape[1]
    nope = w_uk.shape[3]
    h_a = rel_bias.shape[1]
    h_c = b_forget.shape[1]
    ab_cols = w_in_ab.shape[2]
    n_idx_w = (ab_cols - h_a * hd - 2 * hd - idx_dim - q_lora - kv_lora - rope) // (idx_dim + 1)
    past_len = page_table.shape[1] * cache_dsa.shape[2]
    alpha = (2 * depth) ** 0.25
    attn_scale = hd ** -0.5
    mla_scale = (nope + rope) ** -0.5
    idx_w_scale = (n_idx_w * idx_dim) ** -0.5

    c_qa = 0
    c_kv = h_a * hd
    c_qi = c_kv + 2 * hd
    c_tail = c_qi + n_idx_w * idx_dim
    t_wi = idx_dim
    t_cq = t_wi + n_idx_w
    t_ckv = t_cq + q_lora
    t_kr = t_ckv + kv_lora

    x = jnp.concatenate([x_prompt.reshape(m_p, d), x_sample.reshape(m_s, d)], axis=0)
    xb = x.astype(MXU_DTYPE)
    ple = jnp.concatenate([p_prompt.reshape(depth, m_p, -1), p_sample.reshape(depth, m_s, -1)], axis=1)
    pos = jnp.concatenate([jnp.tile(jnp.arange(t_p, dtype=jnp.int32), n_p),
                           past_len + jnp.tile(jnp.arange(t_s, dtype=jnp.int32), n_s)])
    inv = ROPE_THETA ** (-jnp.arange(0, rope, 2, dtype=F32) / rope)
    ang = pos.astype(F32)[:, None] * inv[None, :]
    cos, sin = jnp.cos(ang), jnp.sin(ang)
    bias_p_t = jnp.swapaxes(bias_tables(rel_bias, LANES), 2, 3)
    bias_s = bias_tables(rel_bias, t_s)
    dsa_pool_t = jnp.swapaxes(cache_dsa, 2, 3)
    mla_pool_t = jnp.swapaxes(cache_mla, 2, 3)
    logf_pool_t = jnp.swapaxes(cache_fox_logf, 2, 3)
    w_in_ab_t = jnp.swapaxes(w_in_ab, 1, 2)
    w_in_c_t = jnp.swapaxes(w_in_c, 1, 2)

    new_dsa, new_mla, new_fkv, new_flf = [], [], [], []
    for i in range(depth):
        j = i // 2
        if i % 2 == 0:
            q_a = matmul(xb, w_in_ab_t, layer=j, transposed=True, out_dtype=F32, col_start=c_qa, n_cols=h_a * hd)
            kv_a = matmul(xb, w_in_ab_t, layer=j, transposed=True, out_dtype=F32, col_start=c_kv, n_cols=2 * hd)
            q_i = matmul(xb, w_in_ab_t, layer=j, transposed=True, out_dtype=F32, col_start=c_qi,
                         n_cols=n_idx_w * idx_dim)
            tail = matmul(xb, w_in_ab_t[j, c_tail:, :], transposed=True, out_dtype=F32)
            w_i = tail[:, t_wi:t_cq]
            cqn, mla_rows = ab_post(tail[:, t_cq:t_ckv], tail[:, t_ckv:t_kr], tail[:, t_kr:], cos, sin,
                                    mla_q_norm[j], mla_kv_norm[j])
            dsa_rows = jnp.concatenate([kv_a, tail[:, :t_wi]], axis=-1)
            q_cat = mla_queries(cqn, w_uq[j], w_uk[j], cos, sin)
            w_uv_heads = jnp.transpose(w_uv[j], (1, 0, 2))

            o_a_p = dsa_prompt(q_a, q_i, jnp.transpose(w_i), dsa_rows, jnp.transpose(kv_a[:, hd:]), bias_p_t,
                               n_p, t_p, attn_scale, idx_w_scale, hd, idx_dim)
            o_a_s = dsa_sample(q_a, q_i, w_i, dsa_rows, bias_s, dsa_pool_t, j, page_table, m_p, t_s,
                               attn_scale, idx_w_scale, hd, idx_dim)
            o_b_p = mla_out(mla_prompt(q_cat, mla_rows, jnp.transpose(mla_rows[:, :kv_lora]), n_p, t_p, mla_scale),
                            w_uv_heads)
            o_b_s = mla_out(mla_sample(q_cat, mla_rows, mla_pool_t, j, page_table, kv_lora, m_p, t_s, mla_scale),
                            w_uv_heads)
            heads = jnp.concatenate([jnp.concatenate([o_a_p, o_a_s.astype(MXU_DTYPE)], axis=0),
                                     jnp.concatenate([o_b_p, o_b_s], axis=0)], axis=1)
            mix = matmul(heads, w_o_ab, layer=j, out_dtype=F32)
            new_dsa.append(dsa_rows)
            new_mla.append(mla_rows)
        else:
            q = matmul(xb, w_in_c_t, layer=j, transposed=True, out_dtype=F32, col_start=0, n_cols=h_c * hd)
            kv_c = matmul(xb, w_in_c_t, layer=j, transposed=True, out_dtype=F32, col_start=h_c * hd, n_cols=2 * hd)
            logf = fox_log_forget(xb, w_in_c[j, :, h_c * hd + 2 * hd:], b_forget[j])
            cum_p = cumsum_prompt(logf[:m_p], n_p, t_p)
            cum_t_p = jnp.swapaxes(cum_p.reshape(n_p, t_p, h_c), 1, 2)
            o_p = fox_prompt(q, kv_c, jnp.transpose(kv_c[:, hd:]), cum_p, cum_t_p, n_p, t_p, attn_scale)
            logf_new_t = jnp.pad(jnp.swapaxes(logf[m_p:].reshape(n_s, t_s, h_c), 1, 2),
                                 ((0, 0), (0, 0), (0, logf_pool_t.shape[3] - t_s)))
            o_s = fox_sample(q, kv_c, logf_new_t, cache_fox_kv, logf_pool_t, j, page_table, m_p, t_s, attn_scale)
            heads = jnp.concatenate([o_p, o_s.astype(MXU_DTYPE)], axis=0)
            mix = matmul(heads, w_o_c, layer=j, out_dtype=F32)
            new_fkv.append(kv_c)
            new_flf.append(logf)
        h, hb = deepnorm(x, mix, ln1_g[i], ln1_b[i], alpha)
        if i % 2 == 0:
            ff = matmul(swiglu_up(hb, w_ffn_up, j), w_ffn_down, layer=j, out_dtype=F32)
        else:
            combine = router_combine(hb, w_router[j])
            ff = moe_down(moe_up(hb, w_moe_up, j), w_moe_down, j, jnp.transpose(combine)[:, :, None])
        h, hb = deepnorm(h, ff, ln2_g[i], ln2_b[i], alpha)
        x, xb = ple_mix(h, hb, w_ple_gate, b_ple_gate[i], ple, w_ple_proj, i)

    def split(rows_list):
        a = jnp.stack(rows_list)
        return a[:, :m_p].reshape(len(rows_list), n_p, t_p, -1), a[:, m_p:].reshape(len(rows_list), n_s, t_s, -1)

    dsa_p, dsa_s = split(new_dsa)
    mla_p, mla_s = split(new_mla)
    fkv_p, fkv_s = split(new_fkv)
    flf_p, flf_s = split(new_flf)
    return (x[:m_p].reshape(n_p, t_p, d), x[m_p:].reshape(n_s, t_s, d),
            dsa_p, mla_p, fkv_p, flf_p, dsa_s, mla_s, fkv_s, flf_s)
```
